```python
import jax, jax.numpy as jnp
from jax import lax
import numpy as np

D_MODEL = 2048
BATCH = 1
SEQ = 16384
DEPTH = 2

EPS = 1e-6
NEG_INF = -1e30

HEAD_DIM_A = 64
N_Q_HEADS_A = (D_MODEL // 2) // HEAD_DIM_A
N_KV_HEADS_A = N_Q_HEADS_A // 8
Q_PER_KV_A = N_Q_HEADS_A // N_KV_HEADS_A
WIDTH_A = N_Q_HEADS_A * HEAD_DIM_A
KV_WIDTH_A = N_KV_HEADS_A * HEAD_DIM_A
WINDOW = 128

CHUNK_B = 128
WIDTH_B = D_MODEL // 2
N_GROUPS_B = 8
GROUP_B = WIDTH_B // N_GROUPS_B

HEAD_K_C = 128
HEAD_V_C = 128
WIDTH_C = D_MODEL // 2
N_HEADS_C = WIDTH_C // HEAD_V_C
KEY_WIDTH_C = N_HEADS_C * HEAD_K_C

WIDTH_D = D_MODEL // 2
N_HEADS_D = 4
KEY_WIDTH_D = WIDTH_D // 2
HEAD_K_D = KEY_WIDTH_D // N_HEADS_D
HEAD_V_D = WIDTH_D // N_HEADS_D
GATE_RANK_D = 16
GATE_LOGIT_NORM_D = 16.0

LA_CHUNK = 64
PLE_DIM = 256

N_EVEN = (DEPTH + 1) // 2
N_ODD = DEPTH // 2

EVEN_SPLITS = (WIDTH_A, KV_WIDTH_A, KV_WIDTH_A, WIDTH_A, WIDTH_B, WIDTH_B, WIDTH_B)
ODD_SPLITS = (KEY_WIDTH_C, KEY_WIDTH_C, WIDTH_C, WIDTH_C,
              KEY_WIDTH_D, KEY_WIDTH_D, WIDTH_D, GATE_RANK_D, WIDTH_D)
IN_EVEN = sum(EVEN_SPLITS)
IN_ODD = sum(ODD_SPLITS)
MIX_EVEN = WIDTH_A + WIDTH_B
MIX_ODD = WIDTH_C + WIDTH_D

kernel_name = 'hybrid_swa_gmlp_hgrn2_gla_block'


def rmsnorm(x, g):
    xf = x.astype(jnp.float32)
    y = xf * lax.rsqrt(jnp.mean(xf * xf, axis=-1, keepdims=True) + EPS)
    return (y * g.astype(jnp.float32)).astype(x.dtype)


def head_rmsnorm(o, g, n_heads):
    b, s, w = o.shape
    of = o.astype(jnp.float32).reshape(b, s, n_heads, w // n_heads)
    of = of * lax.rsqrt(jnp.mean(of * of, axis=-1, keepdims=True) + EPS)
    return (of.reshape(b, s, w) * g.astype(jnp.float32)).astype(o.dtype)


def split_cols(z, sizes):
    offs = []
    acc = 0
    for sz in sizes[:-1]:
        acc += sz
        offs.append(acc)
    return jnp.split(z, offs, axis=-1)


def sliding_window_sink_attention(q, k, v, sinks):
    b, s, _, _ = q.shape
    n = s // WINDOW
    qb = q.reshape(b, n, WINDOW, N_KV_HEADS_A, Q_PER_KV_A, HEAD_DIM_A)
    kb = k.reshape(b, n, WINDOW, N_KV_HEADS_A, HEAD_DIM_A)
    vb = v.reshape(b, n, WINDOW, N_KV_HEADS_A, HEAD_DIM_A)
    prev = lambda t: jnp.concatenate([jnp.zeros_like(t[:, :1]), t[:, :-1]], axis=1)
    kk = jnp.concatenate([prev(kb), kb], axis=2)
    vv = jnp.concatenate([prev(vb), vb], axis=2)
    scores = jnp.einsum('bnqhgd,bnkhd->bnhgqk', qb, kk).astype(jnp.float32) * (HEAD_DIM_A ** -0.5)
    qi = jnp.arange(WINDOW)[:, None]
    kj = jnp.arange(2 * WINDOW)[None, :]
    band = (kj > qi) & (kj <= qi + WINDOW)
    valid = (jnp.arange(n)[:, None, None] > 0) | (kj >= WINDOW)[None]
    mask = band[None] & valid
    scores = jnp.where(mask[None, :, None, None], scores, NEG_INF)
    sink = jnp.broadcast_to(
        sinks.astype(jnp.float32).reshape(1, 1, N_KV_HEADS_A, Q_PER_KV_A, 1, 1),
        scores.shape[:-1] + (1,))
    probs = jax.nn.softmax(jnp.concatenate([scores, sink], axis=-1), axis=-1)[..., :-1]
    out = jnp.einsum('bnhgqk,bnkhd->bnqhgd', probs.astype(v.dtype), vv)
    return out.reshape(b, s, WIDTH_A)


def chunked_spatial_gating(u, v, w_s, b_s):
    b, s, _ = u.shape
    n = s // CHUNK_B
    vb = v.reshape(b, n, CHUNK_B, N_GROUPS_B, GROUP_B)
    causal = jnp.tril(jnp.ones((CHUNK_B, CHUNK_B), dtype=bool))
    w = jnp.where(causal[None], w_s, jnp.zeros_like(w_s))
    mixed = jnp.einsum('gts,bnsgc->bntgc', w, vb) + b_s.T[None, None, :, :, None]
    return u * mixed.reshape(b, s, WIDTH_B)


def chunked_gated_linear_attention(q, k, v, log_g, scale):
    b, s, h, dk = q.shape
    dv = v.shape[-1]
    n = s // LA_CHUNK
    to_chunks = lambda t: jnp.moveaxis(t.reshape(b, n, LA_CHUNK, h, t.shape[-1]), 1, 0)
    qc, kc, vc, gc = to_chunks(q * scale), to_chunks(k), to_chunks(v), to_chunks(log_g)
    causal = jnp.tril(jnp.ones((LA_CHUNK, LA_CHUNK), dtype=bool))[None, :, :, None, None]

    def step(state, inp):
        qi, ki, vi, gi = inp
        qf, kf, vf = qi.astype(jnp.float32), ki.astype(jnp.float32), vi.astype(jnp.float32)
        gcum = jnp.cumsum(gi.astype(jnp.float32), axis=1)
        o_inter = jnp.einsum('bthk,bhkv->bthv', qf * jnp.exp(gcum), state)
        diff = gcum[:, :, None] - gcum[:, None, :]
        decay = jnp.where(causal, jnp.exp(jnp.where(causal, diff, 0.0)), 0.0)
        attn = jnp.einsum('bthk,btshk->bhts', qf, decay * kf[:, None])
        o_intra = jnp.einsum('bhts,bshv->bthv', attn, vf)
        g_last = gcum[:, -1]
        k_dec = kf * jnp.exp(g_last[:, None] - gcum)
        new_state = state * jnp.exp(g_last)[..., None] + jnp.einsum('bshk,bshv->bhkv', k_dec, vf)
        return new_state, (o_inter + o_intra).astype(v.dtype)

    state0 = jnp.zeros((b, h, dk, dv), jnp.float32)
    _, out = lax.scan(step, state0, (qc, kc, vc, gc))
    return jnp.moveaxis(out, 0, 1).reshape(b, s, h * dv)


def even_mixer(hn, w_in, sinks, vnorm, w_sp, b_sp, w_out):
    b, s, _ = hn.shape
    z = hn @ w_in
    q, k, v, gate_a, u_b, v_b, gate_b = split_cols(z, EVEN_SPLITS)
    attn = sliding_window_sink_attention(
        q.reshape(b, s, N_Q_HEADS_A, HEAD_DIM_A),
        k.reshape(b, s, N_KV_HEADS_A, HEAD_DIM_A),
        v.reshape(b, s, N_KV_HEADS_A, HEAD_DIM_A), sinks)
    u_b = jax.nn.gelu(u_b, approximate=False)
    v_b = rmsnorm(jax.nn.gelu(v_b, approximate=False), vnorm)
    sgu = chunked_spatial_gating(u_b, v_b, w_sp, b_sp)
    y = jnp.concatenate([attn * jax.nn.silu(gate_a), sgu * jax.nn.silu(gate_b)], axis=-1)
    return y @ w_out


def odd_mixer(hn, layer, w_in, lower_bounds, onorm_c, w_gate_up_d, b_gate_d, onorm_d, w_out):
    b, s, _ = hn.shape
    z = hn @ w_in
    q_c, f_c, i_c, gate_c, q_d, k_d, v_d, glr_d, gate_d = split_cols(z, ODD_SPLITS)
    sm = jax.nn.softmax(lower_bounds.astype(jnp.float32), axis=0)
    lb = (jnp.cumsum(sm, axis=0) - sm[0])[layer].reshape(N_HEADS_C, HEAD_K_C)
    f = lb + (1.0 - lb) * jax.nn.sigmoid(f_c.astype(jnp.float32).reshape(b, s, N_HEADS_C, HEAD_K_C))
    o_c = chunked_gated_linear_attention(
        q_c.reshape(b, s, N_HEADS_C, HEAD_K_C), (1.0 - f).astype(hn.dtype),
        i_c.reshape(b, s, N_HEADS_C, HEAD_V_C), jnp.log(f), HEAD_K_C ** -0.5)
    o_c = head_rmsnorm(o_c, onorm_c, N_HEADS_C) * jax.nn.silu(gate_c)
    log_alpha = jax.nn.log_sigmoid((glr_d @ w_gate_up_d + b_gate_d).astype(jnp.float32)) / GATE_LOGIT_NORM_D
    o_d = chunked_gated_linear_attention(
        q_d.reshape(b, s, N_HEADS_D, HEAD_K_D), k_d.reshape(b, s, N_HEADS_D, HEAD_K_D),
        v_d.reshape(b, s, N_HEADS_D, HEAD_V_D), log_alpha.reshape(b, s, N_HEADS_D, HEAD_K_D),
        HEAD_K_D ** -0.5)
    o_d = head_rmsnorm(o_d, onorm_d, N_HEADS_D) * jax.nn.silu(gate_d)
    return jnp.concatenate([o_c, o_d], axis=-1) @ w_out


def setup_inputs(seed: int = 0) -> dict:
    key = jax.random.key(seed)
    ks = jax.random.split(key, 24)
    f32 = jnp.float32
    nrm = lambda k, shape, scale: jax.random.normal(k, shape, f32) * scale
    gain = lambda k, shape: 1.0 + 0.02 * jax.random.normal(k, shape, f32)
    return {
        'x': nrm(ks[0], (BATCH, SEQ, D_MODEL), 1.0),
        'p': nrm(ks[1], (DEPTH, BATCH, SEQ, PLE_DIM), 1.0),
        'norm_mix': gain(ks[2], (DEPTH, D_MODEL)),
        'w_in_even': nrm(ks[3], (N_EVEN, D_MODEL, IN_EVEN), D_MODEL ** -0.5),
        'sinks_a': nrm(ks[4], (N_EVEN, N_Q_HEADS_A), 0.5),
        'vnorm_b': gain(ks[5], (N_EVEN, WIDTH_B)),
        'w_spatial_b': nrm(ks[6], (N_EVEN, N_GROUPS_B, CHUNK_B, CHUNK_B), CHUNK_B ** -0.5),
        'b_spatial_b': 1.0 + nrm(ks[7], (N_EVEN, N_GROUPS_B, CHUNK_B), 0.02),
        'w_out_even': nrm(ks[8], (N_EVEN, MIX_EVEN, D_MODEL), MIX_EVEN ** -0.5),
        'w_in_odd': nrm(ks[9], (N_ODD, D_MODEL, IN_ODD), D_MODEL ** -0.5),
        'lower_bounds_c': nrm(ks[10], (DEPTH, KEY_WIDTH_C), 0.5),
        'onorm_c': gain(ks[11], (N_ODD, WIDTH_C)),
        'w_gate_up_d': nrm(ks[12], (N_ODD, GATE_RANK_D, KEY_WIDTH_D), GATE_RANK_D ** -0.5),
        'b_gate_d': nrm(ks[13], (N_ODD, KEY_WIDTH_D), 0.1),
        'onorm_d': gain(ks[14], (N_ODD, WIDTH_D)),
        'w_out_odd': nrm(ks[15], (N_ODD, MIX_ODD, D_MODEL), MIX_ODD ** -0.5),
        'w_ple_proj': nrm(ks[16], (DEPTH, PLE_DIM, D_MODEL), PLE_DIM ** -0.5),
        'ple_norm': gain(ks[17], (DEPTH, D_MODEL)),
        'ple_gate_norm': gain(ks[18], (DEPTH, D_MODEL)),
        'w_ple_gate': nrm(ks[19], (DEPTH, D_MODEL, D_MODEL), D_MODEL ** -0.5),
        'final_norm': gain(ks[20], (D_MODEL,)),
    }


def reference(x, p, norm_mix, w_in_even, sinks_a, vnorm_b, w_spatial_b, b_spatial_b, w_out_even,
              w_in_odd, lower_bounds_c, onorm_c, w_gate_up_d, b_gate_d, onorm_d, w_out_odd,
              w_ple_proj, ple_norm, ple_gate_norm, w_ple_gate, final_norm):
    h = x
    for i in range(DEPTH):
        hn = rmsnorm(h, norm_mix[i])
        j = i // 2
        if i % 2 == 0:
            h = h + even_mixer(hn, w_in_even[j], sinks_a[j], vnorm_b[j], w_spatial_b[j],
                               b_spatial_b[j], w_out_even[j])
        else:
            h = h + odd_mixer(hn, i, w_in_odd[j], lower_bounds_c, onorm_c[j], w_gate_up_d[j],
                              b_gate_d[j], onorm_d[j], w_out_odd[j])
        e = rmsnorm(p[i] @ w_ple_proj[i], ple_norm[i])
        gate = jax.nn.sigmoid(rmsnorm(h, ple_gate_norm[i]) @ w_ple_gate[i])
        h = h + e * gate
    return rmsnorm(h, final_norm)
```

```python
import functools

import jax
import jax.numpy as jnp
from jax import lax
from jax.experimental import pallas as pl
from jax.experimental.pallas import tpu as pltpu

F32 = jnp.float32
BF16 = jnp.bfloat16

EPS = 1e-6
NEG_INF = -1e30

D_MODEL = 2048
PLE_DIM = 256
HALF = D_MODEL // 2
HEAD_A = 64
N_KV_A = 2
Q_PER_KV_A = 8
WINDOW = 128
N_GROUPS_B = 8
HEAD_K = 128
N_HEADS_C = 8
N_HEADS_D = 4
HEAD_V_C = 128
HEAD_V_D = 256
GATE_RANK_D = 16
GATE_LOGIT_NORM_D = 16.0
LA_CHUNK = 64
LA_SUB = 16
LANE = 128
VMEM_LIMIT = 56 * 1024 * 1024


def _dot(a, b):
    return jnp.dot(a, b, preferred_element_type=F32)


def _dot_nt(a, b):
    return lax.dot_general(a, b, (((1,), (1,)), ((), ())), preferred_element_type=F32)


def _dot_tn(a, b):
    return lax.dot_general(a, b, (((0,), (0,)), ((), ())), preferred_element_type=F32)


def _rms(x, gain):
    return x * lax.rsqrt(jnp.mean(x * x, axis=-1, keepdims=True) + EPS) * gain


def _silu(x):
    return x * jax.nn.sigmoid(x)


def _gelu(x):
    return 0.5 * x * (1.0 + lax.erf(x * (2.0 ** -0.5)))


def _norm_matmul_kernel(x_ref, g_ref, w_ref, o_ref, xn_ref):
    @pl.when(pl.program_id(1) == 0)
    def _():
        xn_ref[...] = _rms(x_ref[...], g_ref[...]).astype(BF16)

    o_ref[...] = _dot(xn_ref[...], w_ref[...]).astype(o_ref.dtype)


def _norm_matmul(x, gain, w, tm, tn):
    s, d = x.shape
    n = w.shape[1]
    return pl.pallas_call(
        _norm_matmul_kernel,
        grid=(s // tm, n // tn),
        in_specs=[
            pl.BlockSpec((tm, d), lambda i, j: (i, 0)),
            pl.BlockSpec((1, d), lambda i, j: (0, 0)),
            pl.BlockSpec((d, tn), lambda i, j: (0, j)),
        ],
        out_specs=pl.BlockSpec((tm, tn), lambda i, j: (i, j)),
        out_shape=jax.ShapeDtypeStruct((s, n), BF16),
        scratch_shapes=[pltpu.VMEM((tm, d), BF16)],
        compiler_params=pltpu.CompilerParams(
            dimension_semantics=("arbitrary", "arbitrary"), vmem_limit_bytes=VMEM_LIMIT),
        name="norm_matmul",
    )(x, gain.reshape(1, d), w)


def _mixer0_kernel(sinks_ref, q_ref, ga_ref, u_ref, v_ref, gb_ref, k_ref, kp_ref, vv_ref, vp_ref,
                   vn_ref, wsp_ref, bsp_ref, y_ref, *, tb):
    first = pl.program_id(0) == 0
    lane = lax.broadcasted_iota(jnp.int32, (1, LANE), 1)
    lo = lane < HEAD_A
    qi = lax.broadcasted_iota(jnp.int32, (WINDOW, 2 * WINDOW), 0)
    kj = lax.broadcasted_iota(jnp.int32, (WINDOW, 2 * WINDOW), 1)
    band = (kj > qi) & (kj <= qi + WINDOW)
    tri = (lax.broadcasted_iota(jnp.int32, (WINDOW, WINDOW), 0)
           >= lax.broadcasted_iota(jnp.int32, (WINDOW, WINDOW), 1))
    zero_bf = jnp.zeros((), BF16)

    def swap_halves(t):
        return jnp.concatenate([t[:, HEAD_A:], t[:, :HEAD_A]], axis=1)

    for r in range(tb // WINDOW):
        rows = pl.ds(r * WINDOW, WINDOW)
        if r == 0:
            k_prev, v_prev = kp_ref[...], vp_ref[...]
            mask = band & ((kj >= WINDOW) | jnp.logical_not(first))
        else:
            prev = pl.ds((r - 1) * WINDOW, WINDOW)
            k_prev, v_prev = k_ref[prev, :], vv_ref[prev, :]
            mask = band
        kk = jnp.concatenate([k_prev, k_ref[rows, :]], axis=0)
        vv = jnp.concatenate([v_prev, vv_ref[rows, :]], axis=0)
        kk_sw, vv_sw = swap_halves(kk), swap_halves(vv)
        for h in range(N_KV_A):
            k_a, k_b = (kk, kk_sw) if h == 0 else (kk_sw, kk)
            v_a, v_b = (vv, vv_sw) if h == 0 else (vv_sw, vv)
            v2 = jnp.concatenate([jnp.where(lo, v_a, zero_bf), jnp.where(lo, zero_bf, v_b)], axis=0)
            for j in range(Q_PER_KV_A // 2):
                c0 = (h * Q_PER_KV_A + 2 * j) * HEAD_A
                qp = q_ref[rows, pl.ds(c0, LANE)] * jnp.asarray(HEAD_A ** -0.5, BF16)
                ps, rs = [], []
                for half, (qm, kx) in enumerate(((jnp.where(lo, qp, zero_bf), k_a),
                                                 (jnp.where(lo, zero_bf, qp), k_b))):
                    sink = sinks_ref[h * Q_PER_KV_A + 2 * j + half]
                    sc = jnp.where(mask, _dot_nt(qm, kx), NEG_INF)
                    m = jnp.maximum(jnp.max(sc, axis=-1, keepdims=True), sink)
                    p = jnp.exp(sc - m)
                    den = jnp.sum(p, axis=-1, keepdims=True) + jnp.exp(sink - m)
                    ps.append(p.astype(BF16))
                    rs.append(1.0 / den)
                o = _dot(jnp.concatenate(ps, axis=1), v2)
                o = o * jnp.where(lo, rs[0], rs[1])
                g = ga_ref[rows, pl.ds(c0, LANE)].astype(F32)
                y_ref[rows, pl.ds(c0, LANE)] = (o * _silu(g)).astype(BF16)

        u = _gelu(u_ref[rows, :].astype(F32))
        vg = _rms(_gelu(v_ref[rows, :].astype(F32)), vn_ref[...]).astype(BF16)
        gb = gb_ref[rows, :].astype(F32)
        for g in range(N_GROUPS_B):
            cols = slice(g * LANE, (g + 1) * LANE)
            w = jnp.where(tri, wsp_ref[g], 0.0).astype(BF16)
            mixed = _dot(w, vg[:, cols]) + bsp_ref[:, g:g + 1]
            y_ref[rows, pl.ds(HALF + g * LANE, LANE)] = (u[:, cols] * mixed * _silu(gb[:, cols])).astype(BF16)


def _mixer0(z, sinks, vnorm, w_sp, b_sp, tb):
    s = z.shape[0]
    nb = tb // WINDOW
    wide = lambda c: pl.BlockSpec((tb, HALF), lambda i: (i, c))
    kcol, vcol = 5 * HALF // LANE, 5 * HALF // LANE + 1
    cur = lambda c: pl.BlockSpec((tb, LANE), lambda i: (i, c))
    prv = lambda c: pl.BlockSpec((WINDOW, LANE), lambda i: (jnp.maximum(i * nb - 1, 0), c))
    full = lambda a: pl.BlockSpec(a.shape, lambda i: (0,) * a.ndim)
    vnorm = vnorm.reshape(1, HALF)
    b_t = b_sp.T
    return pl.pallas_call(
        functools.partial(_mixer0_kernel, tb=tb),
        grid=(s // tb,),
        in_specs=[pl.BlockSpec(memory_space=pltpu.SMEM),
                  wide(0), wide(1), wide(2), wide(3), wide(4),
                  cur(kcol), prv(kcol), cur(vcol), prv(vcol),
                  full(vnorm), full(w_sp), full(b_t)],
        out_specs=pl.BlockSpec((tb, D_MODEL), lambda i: (i, 0)),
        out_shape=jax.ShapeDtypeStruct((s, D_MODEL), BF16),
        compiler_params=pltpu.CompilerParams(
            dimension_semantics=("arbitrary",), vmem_limit_bytes=VMEM_LIMIT),
        name="mixer0",
    )(sinks, z, z, z, z, z, z, z, z, z, vnorm, w_sp, b_t)


def _chunk_cumsum(log_g, tb):
    t = lax.broadcasted_iota(jnp.int32, (tb, tb), 0)
    s = lax.broadcasted_iota(jnp.int32, (tb, tb), 1)
    tri = jnp.where((s <= t) & (t // LA_CHUNK == s // LA_CHUNK), 1.0, 0.0).astype(BF16)
    hi = log_g.astype(BF16)
    rem = log_g - hi.astype(F32)
    mid = rem.astype(BF16)
    low = (rem - mid.astype(F32)).astype(BF16)
    return _dot(tri, hi) + _dot(tri, mid) + _dot(tri, low)


def _linear_attention_block(qs, k, v, gc, st_ref, kbuf, gbuf, tb):
    pad = LA_SUB
    kbuf[pl.ds(0, pad), :] = jnp.zeros((pad, HEAD_K), F32)
    gbuf[pl.ds(0, pad), :] = jnp.zeros((pad, HEAD_K), F32)
    kbuf[pl.ds(pad, tb), :] = k
    gbuf[pl.ds(pad, tb), :] = gc
    row = lax.broadcasted_iota(jnp.int32, (LA_CHUNK, LA_CHUNK), 0)
    col = lax.broadcasted_iota(jnp.int32, (LA_CHUNK, LA_CHUNK), 1)
    sub_pos = lax.broadcasted_iota(jnp.int32, (LA_CHUNK, 1), 0) % LA_SUB
    zeros = lambda n: jnp.zeros((n, HEAD_K), F32)
    outs = []
    for c in range(tb // LA_CHUNK):
        r0 = c * LA_CHUNK
        qc, kc, g = qs[r0:r0 + LA_CHUNK], k[r0:r0 + LA_CHUNK], gc[r0:r0 + LA_CHUNK]
        vc = v[r0:r0 + LA_CHUNK]
        g_last = g[LA_CHUNK - 1:LA_CHUNK]
        st = st_ref[...]
        o = _dot_nt((qc * jnp.exp(g)).astype(BF16), st.astype(BF16))

        att = jnp.zeros((LA_CHUNK, LA_CHUNK), F32)
        for d in range(LA_SUB):
            ks = kbuf[pl.ds(pad + r0 - d, LA_CHUNK), :]
            gs = gbuf[pl.ds(pad + r0 - d, LA_CHUNK), :]
            pd = jnp.where(sub_pos >= d, qc * ks * jnp.exp(g - gs), 0.0)
            att = jnp.where(col == row - d, jnp.sum(pd, axis=-1, keepdims=True), att)
        m = LA_SUB
        while m < LA_CHUNK:
            q_parts, k_parts = [], []
            for b0 in range(0, LA_CHUNK, 2 * m):
                ref_row = g[b0 + m - 1:b0 + m]
                lower, upper = slice(b0, b0 + m), slice(b0 + m, b0 + 2 * m)
                q_parts += [zeros(m), qc[upper] * jnp.exp(g[upper] - ref_row)]
                k_parts += [kc[lower] * jnp.exp(ref_row - g[lower]), zeros(m)]
            part = _dot_nt(jnp.concatenate(q_parts, axis=0).astype(BF16),
                           jnp.concatenate(k_parts, axis=0).astype(BF16))
            if 2 * m < LA_CHUNK:
                part = jnp.where(row // (2 * m) == col // (2 * m), part, 0.0)
            att = att + part
            m *= 2
        o = o + _dot(att.astype(BF16), vc)

        k_dec = (kc * jnp.exp(g_last - g)).astype(BF16)
        st_ref[...] = st * jnp.exp(g_last) + _dot_tn(vc, k_dec)
        outs.append(o)
    return jnp.concatenate(outs, axis=0)


def _la_scratch(tb, dv):
    return [pltpu.VMEM((dv, HEAD_K), F32),
            pltpu.VMEM((LA_SUB + tb, HEAD_K), F32),
            pltpu.VMEM((LA_SUB + tb, HEAD_K), F32)]


def _hgrn2_kernel(q_ref, f_ref, i_ref, gate_ref, lb_ref, on_ref, y_ref, st_ref, kbuf, gbuf, *, tb):
    @pl.when(pl.program_id(1) == 0)
    def _():
        st_ref[...] = jnp.zeros_like(st_ref)

    lb = lb_ref[...]
    f = lb + (1.0 - lb) * jax.nn.sigmoid(f_ref[...].astype(F32))
    gc = _chunk_cumsum(jnp.log(f), tb)
    qs = q_ref[...].astype(F32) * (HEAD_K ** -0.5)
    o = _linear_attention_block(qs, 1.0 - f, i_ref[...], gc, st_ref, kbuf, gbuf, tb)
    y_ref[...] = (_rms(o, on_ref[...]) * _silu(gate_ref[...].astype(F32))).astype(BF16)


def _gla_kernel(q_ref, k_ref, v_ref, gate_ref, glr_ref, wup_ref, bg_ref, on_ref, y_ref,
                st_ref, kbuf, gbuf, *, tb):
    @pl.when(pl.program_id(1) == 0)
    def _():
        st_ref[...] = jnp.zeros_like(st_ref)

    x = _dot(glr_ref[...], wup_ref[...]) + bg_ref[...]
    log_alpha = -(jnp.maximum(-x, 0.0) + jnp.log1p(jnp.exp(-jnp.abs(x)))) / GATE_LOGIT_NORM_D
    gc = _chunk_cumsum(log_alpha, tb)
    qs = q_ref[...].astype(F32) * (HEAD_K ** -0.5)
    o = _linear_attention_block(qs, k_ref[...].astype(F32), v_ref[...], gc, st_ref, kbuf, gbuf, tb)
    y_ref[...] = (_rms(o, on_ref[...]) * _silu(gate_ref[...].astype(F32))).astype(BF16)


_C_Q, _C_F, _C_I, _C_GATE = 0, 8, 16, 24
_D_Q, _D_K = 32, 36
_D_V, _D_GATE = 20, 24
_D_GLR = 56
PACKED_ODD = 57 * LANE


def _hgrn2(z, lb, onorm, tb):
    s = z.shape[0]
    col = lambda c0: pl.BlockSpec((tb, HEAD_K), lambda h, i: (i, c0 + h))
    per_head = pl.BlockSpec((None, 1, HEAD_K), lambda h, i: (h, 0, 0))
    return pl.pallas_call(
        functools.partial(_hgrn2_kernel, tb=tb),
        grid=(N_HEADS_C, s // tb),
        in_specs=[col(_C_Q), col(_C_F), col(_C_I), col(_C_GATE), per_head, per_head],
        out_specs=pl.BlockSpec((tb, HEAD_V_C), lambda h, i: (i, h)),
        out_shape=jax.ShapeDtypeStruct((s, HALF), BF16),
        scratch_shapes=_la_scratch(tb, HEAD_V_C),
        compiler_params=pltpu.CompilerParams(
            dimension_semantics=("arbitrary", "arbitrary"), vmem_limit_bytes=VMEM_LIMIT),
        name="hgrn2",
    )(z, z, z, z, lb.reshape(N_HEADS_C, 1, HEAD_K), onorm.reshape(N_HEADS_C, 1, HEAD_V_C))


def _gla(z, w_up, b_gate, onorm, tb):
    s = z.shape[0]
    col = lambda c0: pl.BlockSpec((tb, HEAD_K), lambda h, i: (i, c0 + h))
    col2 = lambda c0: pl.BlockSpec((tb, HEAD_V_D), lambda h, i: (i, c0 + h))
    w_up = jnp.pad(w_up, ((0, LANE - GATE_RANK_D), (0, 0))).astype(BF16)
    return pl.pallas_call(
        functools.partial(_gla_kernel, tb=tb),
        grid=(N_HEADS_D, s // tb),
        in_specs=[col(_D_Q), col(_D_K), col2(_D_V), col2(_D_GATE),
                  pl.BlockSpec((tb, LANE), lambda h, i: (i, _D_GLR)),
                  pl.BlockSpec((LANE, HEAD_K), lambda h, i: (0, h)),
                  pl.BlockSpec((None, 1, HEAD_K), lambda h, i: (h, 0, 0)),
                  pl.BlockSpec((None, 1, HEAD_V_D), lambda h, i: (h, 0, 0))],
        out_specs=pl.BlockSpec((tb, HEAD_V_D), lambda h, i: (i, h)),
        out_shape=jax.ShapeDtypeStruct((s, HALF), BF16),
        scratch_shapes=_la_scratch(tb, HEAD_V_D),
        compiler_params=pltpu.CompilerParams(
            dimension_semantics=("arbitrary", "arbitrary"), vmem_limit_bytes=VMEM_LIMIT),
        name="gla",
    )(z, z, z, z, z, w_up, b_gate.reshape(N_HEADS_D, 1, HEAD_K), onorm.reshape(N_HEADS_D, 1, HEAD_V_D))


def _post_kernel(ya_ref, yb_ref, h_ref, p_ref, woa_ref, wob_ref, wple_ref, wg_ref,
                 pn_ref, gn_ref, fn_ref, o_ref, *, final):
    h1 = h_ref[...] + _dot(ya_ref[...], woa_ref[...]) + _dot(yb_ref[...], wob_ref[...])
    e = _rms(_dot(p_ref[...].astype(BF16), wple_ref[...]), pn_ref[...])
    gate = jax.nn.sigmoid(_dot(_rms(h1, gn_ref[...]).astype(BF16), wg_ref[...]))
    h2 = h1 + e * gate
    if final:
        h2 = _rms(h2, fn_ref[...])
    o_ref[...] = h2


def _post(ya, yb, ca, cb, h, p, w_out, w_ple, w_gate, ple_norm, gate_norm, final_norm, tm, final):
    s, d = h.shape
    w_out = w_out.astype(BF16)
    const = lambda shape: pl.BlockSpec(shape, lambda i: (0,) * len(shape), pipeline_mode=pl.Buffered(1))
    vec = lambda a: a.reshape(1, d)
    return pl.pallas_call(
        functools.partial(_post_kernel, final=final),
        grid=(s // tm,),
        in_specs=[pl.BlockSpec((tm, HALF), lambda i: (i, ca)),
                  pl.BlockSpec((tm, HALF), lambda i: (i, cb)),
                  pl.BlockSpec((tm, d), lambda i: (i, 0)),
                  pl.BlockSpec((tm, PLE_DIM), lambda i: (i, 0)),
                  const((HALF, d)), const((HALF, d)), const((PLE_DIM, d)), const((d, d)),
                  const((1, d)), const((1, d)), const((1, d))],
        out_specs=pl.BlockSpec((tm, d), lambda i: (i, 0)),
        out_shape=jax.ShapeDtypeStruct((s, d), F32),
        compiler_params=pltpu.CompilerParams(
            dimension_semantics=("arbitrary",), vmem_limit_bytes=VMEM_LIMIT),
        name="post_final" if final else "post",
    )(ya, yb, h, p, w_out[:HALF], w_out[HALF:], w_ple.astype(BF16), w_gate.astype(BF16),
      vec(ple_norm), vec(gate_norm), vec(final_norm))


def _pack_even(w):
    q, k, v, ga, ub, vb, gb = jnp.split(w, (1024, 1152, 1280, 2304, 3328, 4352), axis=1)
    return jnp.concatenate([q, ga, ub, vb, gb, k, v], axis=1).astype(BF16)


def _pack_odd(w):
    head, glr, gate_d = jnp.split(w, (6144, 6144 + GATE_RANK_D), axis=1)
    pad = jnp.zeros((w.shape[0], LANE - GATE_RANK_D), w.dtype)
    return jnp.concatenate([head, gate_d, glr, pad], axis=1).astype(BF16)


def kernel(x, p, norm_mix, w_in_even, sinks_a, vnorm_b, w_spatial_b, b_spatial_b, w_out_even,
           w_in_odd, lower_bounds_c, onorm_c, w_gate_up_d, b_gate_d, onorm_d, w_out_odd,
           w_ple_proj, ple_norm, ple_gate_norm, w_ple_gate, final_norm):
    assert x.shape[0] == 1 and x.shape[2] == D_MODEL and norm_mix.shape[0] == 2
    s = x.shape[1]
    h = x[0]

    z = _norm_matmul(h, norm_mix[0], _pack_even(w_in_even[0]), tm=min(s, 1024), tn=768)
    y = _mixer0(z, sinks_a[0], vnorm_b[0], w_spatial_b[0], b_spatial_b[0], tb=256)
    h = _post(y, y, 0, 1, h, p[0, 0], w_out_even[0], w_ple_proj[0], w_ple_gate[0],
              ple_norm[0], ple_gate_norm[0], final_norm, tm=256, final=False)

    z = _norm_matmul(h, norm_mix[1], _pack_odd(w_in_odd[0]), tm=min(s, 512), tn=PACKED_ODD // 3)
    sm = jax.nn.softmax(lower_bounds_c.astype(F32), axis=0)
    lb = (jnp.cumsum(sm, axis=0) - sm[0])[1]
    yc = _hgrn2(z, lb, onorm_c[0], tb=256)
    yd = _gla(z, w_gate_up_d[0], b_gate_d[0], onorm_d[0], tb=256)
    out = _post(yc, yd, 0, 0, h, p[1, 0], w_out_odd[0], w_ple_proj[1], w_ple_gate[1],
                ple_norm[1], ple_gate_norm[1], final_norm, tm=256, final=True)
    return out[None]
```

```python
import functools

import jax
import jax.numpy as jnp
from jax import lax
from jax.experimental import pallas as pl
from jax.experimental.pallas import tpu as pltpu

F32 = jnp.float32
BF16 = jnp.bfloat16

EPS = 1e-6
NEG_INF = -1e30

D_MODEL = 2048
PLE_DIM = 256
HALF = D_MODEL // 2
HEAD_A = 64
N_KV_A = 2
Q_PER_KV_A = 8
WINDOW = 128
N_GROUPS_B = 8
HEAD_K = 128
N_HEADS_C = 8
N_HEADS_D = 4
HEAD_V_C = 128
HEAD_V_D = 256
GATE_RANK_D = 16
GATE_LOGIT_NORM_D = 16.0
LA_CHUNK = 64
LA_SUB = 16
MAX_SAFE_STEP_DECAY = 60.0 / LA_SUB
LANE = 128
VMEM_LIMIT = 56 * 1024 * 1024


def _dot(a, b):
    return jnp.dot(a, b, preferred_element_type=F32)


def _dot_nt(a, b):
    return lax.dot_general(a, b, (((1,), (1,)), ((), ())), preferred_element_type=F32)


def _dot_tn(a, b):
    return lax.dot_general(a, b, (((0,), (0,)), ((), ())), preferred_element_type=F32)


def _rms(x, gain):
    return x * lax.rsqrt(jnp.mean(x * x, axis=-1, keepdims=True) + EPS) * gain


def _silu(x):
    return x * jax.nn.sigmoid(x)


def _gelu(x):
    return 0.5 * x * (1.0 + lax.erf(x * (2.0 ** -0.5)))


def _norm_matmul_kernel(x_ref, g_ref, w_ref, o_ref, xn_ref):
    @pl.when(pl.program_id(1) == 0)
    def _():
        xn_ref[...] = _rms(x_ref[...], g_ref[...]).astype(BF16)

    o_ref[...] = _dot(xn_ref[...], w_ref[...]).astype(o_ref.dtype)


def _norm_matmul(x, gain, w, tm, tn):
    s, d = x.shape
    n = w.shape[1]
    return pl.pallas_call(
        _norm_matmul_kernel,
        grid=(s // tm, n // tn),
        in_specs=[
            pl.BlockSpec((tm, d), lambda i, j: (i, 0)),
            pl.BlockSpec((1, d), lambda i, j: (0, 0)),
            pl.BlockSpec((d, tn), lambda i, j: (0, j)),
        ],
        out_specs=pl.BlockSpec((tm, tn), lambda i, j: (i, j)),
        out_shape=jax.ShapeDtypeStruct((s, n), BF16),
        scratch_shapes=[pltpu.VMEM((tm, d), BF16)],
        compiler_params=pltpu.CompilerParams(
            dimension_semantics=("arbitrary", "arbitrary"), vmem_limit_bytes=VMEM_LIMIT),
        name="norm_matmul",
    )(x, gain.reshape(1, d), w)


def _mixer0_kernel(sinks_ref, q_ref, ga_ref, u_ref, v_ref, gb_ref, k_ref, kp_ref, vv_ref, vp_ref,
                   vn_ref, wsp_ref, bsp_ref, y_ref, *, tb):
    first = pl.program_id(0) == 0
    lane = lax.broadcasted_iota(jnp.int32, (1, LANE), 1)
    lo = lane < HEAD_A
    qi = lax.broadcasted_iota(jnp.int32, (WINDOW, 2 * WINDOW), 0)
    kj = lax.broadcasted_iota(jnp.int32, (WINDOW, 2 * WINDOW), 1)
    band = (kj > qi) & (kj <= qi + WINDOW)
    tri = (lax.broadcasted_iota(jnp.int32, (WINDOW, WINDOW), 0)
           >= lax.broadcasted_iota(jnp.int32, (WINDOW, WINDOW), 1))
    zero_bf = jnp.zeros((), BF16)

    def swap_halves(t):
        return jnp.concatenate([t[:, HEAD_A:], t[:, :HEAD_A]], axis=1)

    for r in range(tb // WINDOW):
        rows = pl.ds(r * WINDOW, WINDOW)
        if r == 0:
            k_prev, v_prev = kp_ref[...], vp_ref[...]
            mask = band & ((kj >= WINDOW) | jnp.logical_not(first))
        else:
            prev = pl.ds((r - 1) * WINDOW, WINDOW)
            k_prev, v_prev = k_ref[prev, :], vv_ref[prev, :]
            mask = band
        kk = jnp.concatenate([k_prev, k_ref[rows, :]], axis=0)
        vv = jnp.concatenate([v_prev, vv_ref[rows, :]], axis=0)
        kk_sw, vv_sw = swap_halves(kk), swap_halves(vv)
        for h in range(N_KV_A):
            k_a, k_b = (kk, kk_sw) if h == 0 else (kk_sw, kk)
            v_a, v_b = (vv, vv_sw) if h == 0 else (vv_sw, vv)
            v2 = jnp.concatenate([jnp.where(lo, v_a, zero_bf), jnp.where(lo, zero_bf, v_b)], axis=0)
            for j in range(Q_PER_KV_A // 2):
                c0 = (h * Q_PER_KV_A + 2 * j) * HEAD_A
                qp = q_ref[rows, pl.ds(c0, LANE)] * jnp.asarray(HEAD_A ** -0.5, BF16)
                ps, rs = [], []
                for half, (qm, kx) in enumerate(((jnp.where(lo, qp, zero_bf), k_a),
                                                 (jnp.where(lo, zero_bf, qp), k_b))):
                    sink = sinks_ref[h * Q_PER_KV_A + 2 * j + half]
                    sc = jnp.where(mask, _dot_nt(qm, kx), NEG_INF)
                    m = jnp.maximum(jnp.max(sc, axis=-1, keepdims=True), sink)
                    p = jnp.exp(sc - m)
                    den = jnp.sum(p, axis=-1, keepdims=True) + jnp.exp(sink - m)
                    ps.append(p.astype(BF16))
                    rs.append(1.0 / den)
                o = _dot(jnp.concatenate(ps, axis=1), v2)
                o = o * jnp.where(lo, rs[0], rs[1])
                g = ga_ref[rows, pl.ds(c0, LANE)].astype(F32)
                y_ref[rows, pl.ds(c0, LANE)] = (o * _silu(g)).astype(BF16)

        u = _gelu(u_ref[rows, :].astype(F32))
        vg = _rms(_gelu(v_ref[rows, :].astype(F32)), vn_ref[...]).astype(BF16)
        gb = gb_ref[rows, :].astype(F32)
        for g in range(N_GROUPS_B):
            cols = slice(g * LANE, (g + 1) * LANE)
            w = jnp.where(tri, wsp_ref[g], 0.0).astype(BF16)
            mixed = _dot(w, vg[:, cols]) + bsp_ref[:, g:g + 1]
            y_ref[rows, pl.ds(HALF + g * LANE, LANE)] = (u[:, cols] * mixed * _silu(gb[:, cols])).astype(BF16)


def _mixer0(z, sinks, vnorm, w_sp, b_sp, tb):
    s = z.shape[0]
    nb = tb // WINDOW
    wide = lambda c: pl.BlockSpec((tb, HALF), lambda i: (i, c))
    kcol, vcol = 5 * HALF // LANE, 5 * HALF // LANE + 1
    cur = lambda c: pl.BlockSpec((tb, LANE), lambda i: (i, c))
    prv = lambda c: pl.BlockSpec((WINDOW, LANE), lambda i: (jnp.maximum(i * nb - 1, 0), c))
    full = lambda a: pl.BlockSpec(a.shape, lambda i: (0,) * a.ndim)
    vnorm = vnorm.reshape(1, HALF)
    b_t = b_sp.T
    return pl.pallas_call(
        functools.partial(_mixer0_kernel, tb=tb),
        grid=(s // tb,),
        in_specs=[pl.BlockSpec(memory_space=pltpu.SMEM),
                  wide(0), wide(1), wide(2), wide(3), wide(4),
                  cur(kcol), prv(kcol), cur(vcol), prv(vcol),
                  full(vnorm), full(w_sp), full(b_t)],
        out_specs=pl.BlockSpec((tb, D_MODEL), lambda i: (i, 0)),
        out_shape=jax.ShapeDtypeStruct((s, D_MODEL), BF16),
        compiler_params=pltpu.CompilerParams(
            dimension_semantics=("arbitrary",), vmem_limit_bytes=VMEM_LIMIT),
        name="mixer0",
    )(sinks, z, z, z, z, z, z, z, z, z, vnorm, w_sp, b_t)


def _chunk_cumsum(log_g, tb):
    t = lax.broadcasted_iota(jnp.int32, (tb, tb), 0)
    s = lax.broadcasted_iota(jnp.int32, (tb, tb), 1)
    tri = jnp.where((s <= t) & (t // LA_CHUNK == s // LA_CHUNK), 1.0, 0.0).astype(BF16)
    hi = log_g.astype(BF16)
    rem = log_g - hi.astype(F32)
    mid = rem.astype(BF16)
    low = (rem - mid.astype(F32)).astype(BF16)
    return _dot(tri, hi) + _dot(tri, mid) + _dot(tri, low)


def _att_exact(qc, g, kbuf, gbuf, r0):
    row = lax.broadcasted_iota(jnp.int32, (LA_CHUNK, LA_CHUNK), 0)
    col = lax.broadcasted_iota(jnp.int32, (LA_CHUNK, LA_CHUNK), 1)
    sub_pos = lax.broadcasted_iota(jnp.int32, (LA_CHUNK, 1), 0) % LA_SUB
    zeros = lambda n: jnp.zeros((n, HEAD_K), F32)
    kc = kbuf[pl.ds(LA_SUB + r0, LA_CHUNK), :]
    att = jnp.zeros((LA_CHUNK, LA_CHUNK), F32)
    for d in range(LA_SUB):
        ks = kbuf[pl.ds(LA_SUB + r0 - d, LA_CHUNK), :]
        gs = gbuf[pl.ds(LA_SUB + r0 - d, LA_CHUNK), :]
        pd = jnp.where(sub_pos >= d, qc * ks * jnp.exp(g - gs), 0.0)
        att = jnp.where(col == row - d, jnp.sum(pd, axis=-1, keepdims=True), att)
    m = LA_SUB
    while m < LA_CHUNK:
        q_parts, k_parts = [], []
        for b0 in range(0, LA_CHUNK, 2 * m):
            ref_row = g[b0 + m - 1:b0 + m]
            lower, upper = slice(b0, b0 + m), slice(b0 + m, b0 + 2 * m)
            q_parts += [zeros(m), qc[upper] * jnp.exp(g[upper] - ref_row)]
            k_parts += [kc[lower] * jnp.exp(ref_row - g[lower]), zeros(m)]
        part = _dot_nt(jnp.concatenate(q_parts, axis=0).astype(BF16),
                       jnp.concatenate(k_parts, axis=0).astype(BF16))
        if 2 * m < LA_CHUNK:
            part = jnp.where(row // (2 * m) == col // (2 * m), part, 0.0)
        att = att + part
        m *= 2
    return att


def _att_factored(qc, kc, g):
    row = lax.broadcasted_iota(jnp.int32, (LA_CHUNK, LA_CHUNK), 0)
    col = lax.broadcasted_iota(jnp.int32, (LA_CHUNK, LA_CHUNK), 1)
    zeros = lambda n: jnp.zeros((n, HEAD_K), BF16)
    q_slabs, k_slabs = [], []
    for b0 in range(0, LA_CHUNK, LA_SUB):
        blk = slice(b0, b0 + LA_SUB)
        ref_row = g[b0 - 1:b0] if b0 else jnp.zeros((1, HEAD_K), F32)
        q_part = (qc[b0:] * jnp.exp(g[b0:] - ref_row)).astype(BF16)
        k_part = (kc[blk] * jnp.exp(ref_row - g[blk])).astype(BF16)
        q_slabs.append(jnp.concatenate([zeros(b0), q_part], axis=0) if b0 else q_part)
        k_slabs.append(jnp.concatenate(
            [z for z in (zeros(b0), k_part, zeros(LA_CHUNK - LA_SUB - b0)) if z.shape[0]], axis=0))
    att = _dot_nt(jnp.concatenate(q_slabs, axis=1), jnp.concatenate(k_slabs, axis=1))
    return jnp.where(col <= row, att, 0.0)


def _linear_attention_heads(qs, ks, vs, log_gs, st_ref, kbuf, gbuf, abuf, tb):
    n_heads, n_chunks = len(qs), tb // LA_CHUNK
    chunk = lambda a, c: a[c * LA_CHUNK:(c + 1) * LA_CHUNK]
    arows = lambda j, c: pl.ds((j * n_chunks + c) * LA_CHUNK, LA_CHUNK)
    gcs = [_chunk_cumsum(lg, tb) for lg in log_gs]
    steepest = functools.reduce(jnp.maximum, [-lg for lg in log_gs])
    safe = jnp.max(steepest) <= MAX_SAFE_STEP_DECAY

    @pl.when(safe)
    def _():
        for j in range(n_heads):
            for c in range(n_chunks):
                abuf[arows(j, c), :] = _att_factored(chunk(qs[j], c), chunk(ks[j], c), chunk(gcs[j], c))

    @pl.when(jnp.logical_not(safe))
    def _():
        for j in range(n_heads):
            kbuf[j, pl.ds(0, LA_SUB), :] = jnp.zeros((LA_SUB, HEAD_K), F32)
            gbuf[j, pl.ds(0, LA_SUB), :] = jnp.zeros((LA_SUB, HEAD_K), F32)
            kbuf[j, pl.ds(LA_SUB, tb), :] = ks[j]
            gbuf[j, pl.ds(LA_SUB, tb), :] = gcs[j]
            for c in range(n_chunks):
                abuf[arows(j, c), :] = _att_exact(
                    chunk(qs[j], c), chunk(gcs[j], c), kbuf.at[j], gbuf.at[j], c * LA_CHUNK)

    intra, updates, decays = {}, {}, {}
    for j in range(n_heads):
        for c in range(n_chunks):
            kc, vc, g = chunk(ks[j], c), chunk(vs[j], c), chunk(gcs[j], c)
            g_last = g[LA_CHUNK - 1:LA_CHUNK]
            intra[j, c] = _dot(abuf[arows(j, c), :].astype(BF16), vc)
            updates[j, c] = _dot_tn(vc, (kc * jnp.exp(g_last - g)).astype(BF16))
            decays[j, c] = jnp.exp(g_last)
    states = {}
    for j in range(n_heads):
        st = st_ref[j]
        for c in range(n_chunks):
            states[j, c] = st.astype(BF16)
            st = st * decays[j, c] + updates[j, c]
        st_ref[j] = st
    return [jnp.concatenate(
        [_dot_nt((chunk(qs[j], c) * jnp.exp(chunk(gcs[j], c))).astype(BF16), states[j, c]) + intra[j, c]
         for c in range(n_chunks)], axis=0) for j in range(n_heads)]


def _la_scratch(tb, dv, hp):
    return [pltpu.VMEM((hp, dv, HEAD_K), F32),
            pltpu.VMEM((hp, LA_SUB + tb, HEAD_K), F32),
            pltpu.VMEM((hp, LA_SUB + tb, HEAD_K), F32),
            pltpu.VMEM((hp * tb, LA_CHUNK), F32)]


def _head_cols(j, width):
    return slice(j * width, (j + 1) * width)


def _hgrn2_kernel(q_ref, f_ref, i_ref, gate_ref, lb_ref, on_ref, y_ref, st_ref, kbuf, gbuf, abuf, *, tb, hp):
    @pl.when(pl.program_id(1) == 0)
    def _():
        st_ref[...] = jnp.zeros_like(st_ref)

    qs, ks, vs, lgs = [], [], [], []
    for j in range(hp):
        cols = _head_cols(j, HEAD_K)
        lb = lb_ref[:, cols]
        f = lb + (1.0 - lb) * jax.nn.sigmoid(f_ref[:, cols].astype(F32))
        qs.append(q_ref[:, cols].astype(F32) * (HEAD_K ** -0.5))
        ks.append(1.0 - f)
        vs.append(i_ref[:, cols])
        lgs.append(jnp.log(f))
    outs = _linear_attention_heads(qs, ks, vs, lgs, st_ref, kbuf, gbuf, abuf, tb)
    for j in range(hp):
        cols = _head_cols(j, HEAD_V_C)
        y_ref[:, cols] = (_rms(outs[j], on_ref[:, cols]) * _silu(gate_ref[:, cols].astype(F32))).astype(BF16)


def _gla_kernel(q_ref, k_ref, v_ref, gate_ref, glr_ref, wup_ref, bg_ref, on_ref, y_ref,
                st_ref, kbuf, gbuf, abuf, *, tb, hp):
    @pl.when(pl.program_id(1) == 0)
    def _():
        st_ref[...] = jnp.zeros_like(st_ref)

    x = _dot(glr_ref[...], wup_ref[...]) + bg_ref[...]
    log_alpha = -(jnp.maximum(-x, 0.0) + jnp.log1p(jnp.exp(-jnp.abs(x)))) / GATE_LOGIT_NORM_D
    qs, ks, vs, lgs = [], [], [], []
    for j in range(hp):
        cols = _head_cols(j, HEAD_K)
        qs.append(q_ref[:, cols].astype(F32) * (HEAD_K ** -0.5))
        ks.append(k_ref[:, cols].astype(F32))
        vs.append(v_ref[:, _head_cols(j, HEAD_V_D)])
        lgs.append(log_alpha[:, cols])
    outs = _linear_attention_heads(qs, ks, vs, lgs, st_ref, kbuf, gbuf, abuf, tb)
    for j in range(hp):
        cols = _head_cols(j, HEAD_V_D)
        y_ref[:, cols] = (_rms(outs[j], on_ref[:, cols]) * _silu(gate_ref[:, cols].astype(F32))).astype(BF16)


_C_Q, _C_F, _C_I, _C_GATE = 0, 8, 16, 24
_D_Q, _D_K, _D_V, _D_GATE, _D_GLR = 32, 36, 40, 48, 56
PACKED_ODD = 57 * LANE


def _hgrn2(z, lb, onorm, tb, hp):
    s = z.shape[0]
    kw, vw = hp * HEAD_K, hp * HEAD_V_C
    col = lambda c0: pl.BlockSpec((tb, kw), lambda h, i: (i, c0 // hp + h))
    vec = lambda w: pl.BlockSpec((1, w), lambda h, i: (0, h))
    return pl.pallas_call(
        functools.partial(_hgrn2_kernel, tb=tb, hp=hp),
        grid=(N_HEADS_C // hp, s // tb),
        in_specs=[col(_C_Q), col(_C_F), col(_C_I), col(_C_GATE), vec(kw), vec(vw)],
        out_specs=pl.BlockSpec((tb, vw), lambda h, i: (i, h)),
        out_shape=jax.ShapeDtypeStruct((s, HALF), BF16),
        scratch_shapes=_la_scratch(tb, HEAD_V_C, hp),
        compiler_params=pltpu.CompilerParams(
            dimension_semantics=("arbitrary", "arbitrary"), vmem_limit_bytes=VMEM_LIMIT),
        name="hgrn2",
    )(z, z, z, z, lb.reshape(1, HALF), onorm.reshape(1, HALF))


def _gla(z, w_up, b_gate, onorm, tb, hp):
    s = z.shape[0]
    kw, vw = hp * HEAD_K, hp * HEAD_V_D
    col = lambda c0, w: pl.BlockSpec((tb, w), lambda h, i: (i, c0 * LANE // w + h))
    vec = lambda w: pl.BlockSpec((1, w), lambda h, i: (0, h))
    w_up = jnp.pad(w_up, ((0, LANE - GATE_RANK_D), (0, 0))).astype(BF16)
    return pl.pallas_call(
        functools.partial(_gla_kernel, tb=tb, hp=hp),
        grid=(N_HEADS_D // hp, s // tb),
        in_specs=[col(_D_Q, kw), col(_D_K, kw), col(_D_V, vw), col(_D_GATE, vw),
                  pl.BlockSpec((tb, LANE), lambda h, i: (i, _D_GLR)),
                  pl.BlockSpec((LANE, kw), lambda h, i: (0, h)),
                  vec(kw), vec(vw)],
        out_specs=pl.BlockSpec((tb, vw), lambda h, i: (i, h)),
        out_shape=jax.ShapeDtypeStruct((s, HALF), BF16),
        scratch_shapes=_la_scratch(tb, HEAD_V_D, hp),
        compiler_params=pltpu.CompilerParams(
            dimension_semantics=("arbitrary", "arbitrary"), vmem_limit_bytes=VMEM_LIMIT),
        name="gla",
    )(z, z, z, z, z, w_up, b_gate.reshape(1, N_HEADS_D * HEAD_K), onorm.reshape(1, HALF))


def _post_kernel(ya_ref, yb_ref, h_ref, p_ref, woa_ref, wob_ref, wple_ref, wg_ref,
                 pn_ref, gn_ref, fn_ref, o_ref, *, final):
    h1 = h_ref[...] + _dot(ya_ref[...], woa_ref[...]) + _dot(yb_ref[...], wob_ref[...])
    e = _rms(_dot(p_ref[...].astype(BF16), wple_ref[...]), pn_ref[...])
    gate = jax.nn.sigmoid(_dot(_rms(h1, gn_ref[...]).astype(BF16), wg_ref[...]))
    h2 = h1 + e * gate
    if final:
        h2 = _rms(h2, fn_ref[...])
    o_ref[...] = h2


def _post(ya, yb, ca, cb, h, p, w_out, w_ple, w_gate, ple_norm, gate_norm, final_norm, tm, final):
    s, d = h.shape
    w_out_bf = w_out.astype(BF16)
    const = lambda shape, r=0: pl.BlockSpec(shape, lambda i: (r, 0), pipeline_mode=pl.Buffered(1))
    vec = lambda a: a.reshape(1, d)
    return pl.pallas_call(
        functools.partial(_post_kernel, final=final),
        grid=(s // tm,),
        in_specs=[pl.BlockSpec((tm, HALF), lambda i: (i, ca)),
                  pl.BlockSpec((tm, HALF), lambda i: (i, cb)),
                  pl.BlockSpec((tm, d), lambda i: (i, 0)),
                  pl.BlockSpec((tm, PLE_DIM), lambda i: (i, 0)),
                  const((HALF, d), 0), const((HALF, d), 1), const((PLE_DIM, d)), const((d, d)),
                  const((1, d)), const((1, d)), const((1, d))],
        out_specs=pl.BlockSpec((tm, d), lambda i: (i, 0)),
        out_shape=jax.ShapeDtypeStruct((s, d), F32),
        compiler_params=pltpu.CompilerParams(
            dimension_semantics=("arbitrary",), vmem_limit_bytes=VMEM_LIMIT),
        name="post_final" if final else "post",
    )(ya, yb, h, p, w_out_bf, w_out_bf, w_ple.astype(BF16), w_gate.astype(BF16),
      vec(ple_norm), vec(gate_norm), vec(final_norm))


def _pack_even(w):
    w = w.astype(BF16)
    return jnp.concatenate([w[:, :1024], w[:, 1280:], w[:, 1024:1280]], axis=1)


def _pack_odd(w):
    w = w.astype(BF16)
    glr_end = 6144 + GATE_RANK_D
    pad = jnp.zeros((w.shape[0], LANE - GATE_RANK_D), BF16)
    return jnp.concatenate([w[:, :6144], w[:, glr_end:], w[:, 6144:glr_end], pad], axis=1)


def kernel(x, p, norm_mix, w_in_even, sinks_a, vnorm_b, w_spatial_b, b_spatial_b, w_out_even,
           w_in_odd, lower_bounds_c, onorm_c, w_gate_up_d, b_gate_d, onorm_d, w_out_odd,
           w_ple_proj, ple_norm, ple_gate_norm, w_ple_gate, final_norm):
    assert x.shape[0] == 1 and x.shape[2] == D_MODEL and norm_mix.shape[0] == 2
    s = x.shape[1]
    h = x[0]

    z = _norm_matmul(h, norm_mix[0], _pack_even(w_in_even[0]), tm=min(s, 1024), tn=768)
    y = _mixer0(z, sinks_a[0], vnorm_b[0], w_spatial_b[0], b_spatial_b[0], tb=256)
    h = _post(y, y, 0, 1, h, p[0, 0], w_out_even[0], w_ple_proj[0], w_ple_gate[0],
              ple_norm[0], ple_gate_norm[0], final_norm, tm=256, final=False)

    z = _norm_matmul(h, norm_mix[1], _pack_odd(w_in_odd[0]), tm=min(s, 512), tn=PACKED_ODD // 3)
    sm = jax.nn.softmax(lower_bounds_c.astype(F32), axis=0)
    lb = (jnp.cumsum(sm, axis=0) - sm[0])[1]
    yc = _hgrn2(z, lb, onorm_c[0], tb=256, hp=4)
    yd = _gla(z, w_gate_up_d[0], b_gate_d[0], onorm_d[0], tb=256, hp=2)
    out = _post(yc, yd, 0, 0, h, p[1, 0], w_out_odd[0], w_ple_proj[1], w_ple_gate[1],
                ple_norm[1], ple_gate_norm[1], final_norm, tm=256, final=True)
    return out[None]
```

```python
import functools

import jax
import jax.numpy as jnp
from jax import lax
from jax.experimental import pallas as pl
from jax.experimental.pallas import tpu as pltpu

F32 = jnp.float32
BF16 = jnp.bfloat16

EPS = 1e-6
NEG_INF = -1e30

D_MODEL = 2048
PLE_DIM = 256
HALF = D_MODEL // 2
HEAD_A = 64
N_KV_A = 2
Q_PER_KV_A = 8
WINDOW = 128
N_GROUPS_B = 8
HEAD_K = 128
N_HEADS_C = 8
N_HEADS_D = 4
HEAD_V_C = 128
HEAD_V_D = 256
GATE_RANK_D = 16
GATE_LOGIT_NORM_D = 16.0
LA_CHUNK = 64
LA_SUB = 16
MAX_SAFE_STEP_DECAY = 60.0 / LA_SUB
LANE = 128
VMEM_LIMIT = 56 * 1024 * 1024


def _dot(a, b):
    return jnp.dot(a, b, preferred_element_type=F32)


def _dot_nt(a, b):
    return lax.dot_general(a, b, (((1,), (1,)), ((), ())), preferred_element_type=F32)


def _dot_tn(a, b):
    return lax.dot_general(a, b, (((0,), (0,)), ((), ())), preferred_element_type=F32)


def _rms(x, gain):
    return x * lax.rsqrt(jnp.mean(x * x, axis=-1, keepdims=True) + EPS) * gain


def _silu(x):
    return x * jax.nn.sigmoid(x)


def _gelu(x):
    return 0.5 * x * (1.0 + lax.erf(x * (2.0 ** -0.5)))


def _norm_matmul_kernel(x_ref, g_ref, w_ref, o_ref, xn_ref):
    @pl.when(pl.program_id(1) == 0)
    def _():
        xn_ref[...] = _rms(x_ref[...], g_ref[...]).astype(BF16)

    o_ref[...] = _dot(xn_ref[...], w_ref[...]).astype(o_ref.dtype)


def _norm_matmul(x, gain, w, tm, tn):
    s, d = x.shape
    n = w.shape[1]
    return pl.pallas_call(
        _norm_matmul_kernel,
        grid=(s // tm, n // tn),
        in_specs=[
            pl.BlockSpec((tm, d), lambda i, j: (i, 0)),
            pl.BlockSpec((1, d), lambda i, j: (0, 0)),
            pl.BlockSpec((d, tn), lambda i, j: (0, j)),
        ],
        out_specs=pl.BlockSpec((tm, tn), lambda i, j: (i, j)),
        out_shape=jax.ShapeDtypeStruct((s, n), BF16),
        scratch_shapes=[pltpu.VMEM((tm, d), BF16)],
        compiler_params=pltpu.CompilerParams(
            dimension_semantics=("arbitrary", "arbitrary"), vmem_limit_bytes=VMEM_LIMIT),
        name="norm_matmul",
    )(x, gain.reshape(1, d), w)


_A_Q, _A_K, _A_V, _A_GATE, _B_U, _B_V, _B_GATE = 0, 1024, 1152, 1280, 2304, 3328, 4352
IN_EVEN = 5376


def _mixer0_kernel(sinks_ref, z_ref, kp_ref, vp_ref, vn_ref, wsp_ref, bsp_ref, y_ref, *, tb):
    first = pl.program_id(0) == 0
    lane = lax.broadcasted_iota(jnp.int32, (1, LANE), 1)
    lo = lane < HEAD_A
    qi = lax.broadcasted_iota(jnp.int32, (WINDOW, 2 * WINDOW), 0)
    kj = lax.broadcasted_iota(jnp.int32, (WINDOW, 2 * WINDOW), 1)
    band = (kj > qi) & (kj <= qi + WINDOW)
    tri = (lax.broadcasted_iota(jnp.int32, (WINDOW, WINDOW), 0)
           >= lax.broadcasted_iota(jnp.int32, (WINDOW, WINDOW), 1))
    zero_bf = jnp.zeros((), BF16)
    n_q = Q_PER_KV_A

    def swap_halves(t):
        return jnp.concatenate([t[:, HEAD_A:], t[:, :HEAD_A]], axis=1)

    def both_halves(t, t_sw, h):
        return jnp.where(lo, t, t_sw) if h == 0 else jnp.where(lo, t_sw, t)

    for r in range(tb // WINDOW):
        rows = pl.ds(r * WINDOW, WINDOW)
        zcols = lambda c0, w: z_ref[rows, pl.ds(c0, w)]
        if r == 0:
            k_prev, v_prev = kp_ref[...], vp_ref[...]
            mask = band & ((kj >= WINDOW) | jnp.logical_not(first))
        else:
            prev = pl.ds((r - 1) * WINDOW, WINDOW)
            k_prev, v_prev = z_ref[prev, pl.ds(_A_K, LANE)], z_ref[prev, pl.ds(_A_V, LANE)]
            mask = band
        kk = jnp.concatenate([k_prev, zcols(_A_K, LANE)], axis=0)
        vv = jnp.concatenate([v_prev, zcols(_A_V, LANE)], axis=0)
        kk_sw, vv_sw = swap_halves(kk), swap_halves(vv)

        scores = []
        for h in range(N_KV_A):
            parts = []
            for j in range(n_q // 2):
                qp = zcols(_A_Q + (h * n_q + 2 * j) * HEAD_A, LANE) * jnp.asarray(HEAD_A ** -0.5, BF16)
                parts += [jnp.where(lo, qp, zero_bf), jnp.where(lo, zero_bf, qp)]
            scores.append(_dot_nt(jnp.concatenate(parts, axis=0), both_halves(kk, kk_sw, h)))

        u = _gelu(zcols(_B_U, HALF).astype(F32))
        vg = _rms(_gelu(zcols(_B_V, HALF).astype(F32)), vn_ref[...]).astype(BF16)
        gb = zcols(_B_GATE, HALF).astype(F32)
        for g in range(N_GROUPS_B):
            cols = slice(g * LANE, (g + 1) * LANE)
            w = jnp.where(tri, wsp_ref[g], 0.0).astype(BF16)
            mixed = _dot(w, vg[:, cols]) + bsp_ref[:, g:g + 1]
            y_ref[rows, pl.ds(HALF + g * LANE, LANE)] = (u[:, cols] * mixed * _silu(gb[:, cols])).astype(BF16)

        for h in range(N_KV_A):
            ps, rs = [], []
            for i in range(n_q):
                sink = sinks_ref[h * n_q + i]
                sc = jnp.where(mask, scores[h][i * WINDOW:(i + 1) * WINDOW], NEG_INF)
                m = jnp.maximum(jnp.max(sc, axis=-1, keepdims=True), sink)
                p = jnp.exp(sc - m)
                rs.append(1.0 / (jnp.sum(p, axis=-1, keepdims=True) + jnp.exp(sink - m)))
                ps.append(p.astype(BF16))
            v_both = both_halves(vv, vv_sw, h)
            v2 = jnp.concatenate([jnp.where(lo, v_both, zero_bf), jnp.where(lo, zero_bf, v_both)], axis=0)
            p_pairs = jnp.concatenate(
                [jnp.concatenate(ps[2 * j:2 * j + 2], axis=1) for j in range(n_q // 2)], axis=0)
            o = _dot(p_pairs, v2)
            for j in range(n_q // 2):
                c0 = (h * n_q + 2 * j) * HEAD_A
                oj = o[j * WINDOW:(j + 1) * WINDOW] * jnp.where(lo, rs[2 * j], rs[2 * j + 1])
                g = zcols(_A_GATE + c0, LANE).astype(F32)
                y_ref[rows, pl.ds(c0, LANE)] = (oj * _silu(g)).astype(BF16)


def _mixer0(z, sinks, vnorm, w_sp, b_sp, tb):
    s = z.shape[0]
    nb = tb // WINDOW
    prv = lambda c0: pl.BlockSpec((WINDOW, LANE), lambda i: (jnp.maximum(i * nb - 1, 0), c0 // LANE))
    full = lambda a: pl.BlockSpec(a.shape, lambda i: (0,) * a.ndim)
    vnorm = vnorm.reshape(1, HALF)
    b_t = b_sp.T
    return pl.pallas_call(
        functools.partial(_mixer0_kernel, tb=tb),
        grid=(s // tb,),
        in_specs=[pl.BlockSpec(memory_space=pltpu.SMEM),
                  pl.BlockSpec((tb, IN_EVEN), lambda i: (i, 0)), prv(_A_K), prv(_A_V),
                  full(vnorm), full(w_sp), full(b_t)],
        out_specs=pl.BlockSpec((tb, D_MODEL), lambda i: (i, 0)),
        out_shape=jax.ShapeDtypeStruct((s, D_MODEL), BF16),
        compiler_params=pltpu.CompilerParams(
            dimension_semantics=("arbitrary",), vmem_limit_bytes=VMEM_LIMIT),
        name="mixer0",
    )(sinks, z, z, z, vnorm, w_sp, b_t)


def _chunk_cumsum(log_g, tb):
    t = lax.broadcasted_iota(jnp.int32, (tb, tb), 0)
    s = lax.broadcasted_iota(jnp.int32, (tb, tb), 1)
    tri = jnp.where((s <= t) & (t // LA_CHUNK == s // LA_CHUNK), 1.0, 0.0).astype(BF16)
    hi = log_g.astype(BF16)
    rem = log_g - hi.astype(F32)
    mid = rem.astype(BF16)
    low = (rem - mid.astype(F32)).astype(BF16)
    return _dot(tri, hi) + _dot(tri, mid) + _dot(tri, low)


def _att_exact(qc, g, kbuf, gbuf, r0):
    row = lax.broadcasted_iota(jnp.int32, (LA_CHUNK, LA_CHUNK), 0)
    col = lax.broadcasted_iota(jnp.int32, (LA_CHUNK, LA_CHUNK), 1)
    sub_pos = lax.broadcasted_iota(jnp.int32, (LA_CHUNK, 1), 0) % LA_SUB
    zeros = lambda n: jnp.zeros((n, HEAD_K), F32)
    kc = kbuf[pl.ds(LA_SUB + r0, LA_CHUNK), :]
    att = jnp.zeros((LA_CHUNK, LA_CHUNK), F32)
    for d in range(LA_SUB):
        ks = kbuf[pl.ds(LA_SUB + r0 - d, LA_CHUNK), :]
        gs = gbuf[pl.ds(LA_SUB + r0 - d, LA_CHUNK), :]
        pd = jnp.where(sub_pos >= d, qc * ks * jnp.exp(g - gs), 0.0)
        att = jnp.where(col == row - d, jnp.sum(pd, axis=-1, keepdims=True), att)
    m = LA_SUB
    while m < LA_CHUNK:
        q_parts, k_parts = [], []
        for b0 in range(0, LA_CHUNK, 2 * m):
            ref_row = g[b0 + m - 1:b0 + m]
            lower, upper = slice(b0, b0 + m), slice(b0 + m, b0 + 2 * m)
            q_parts += [zeros(m), qc[upper] * jnp.exp(g[upper] - ref_row)]
            k_parts += [kc[lower] * jnp.exp(ref_row - g[lower]), zeros(m)]
        part = _dot_nt(jnp.concatenate(q_parts, axis=0).astype(BF16),
                       jnp.concatenate(k_parts, axis=0).astype(BF16))
        if 2 * m < LA_CHUNK:
            part = jnp.where(row // (2 * m) == col // (2 * m), part, 0.0)
        att = att + part
        m *= 2
    return att


def _att_factored(qc, kc, g):
    row = lax.broadcasted_iota(jnp.int32, (LA_CHUNK, LA_CHUNK), 0)
    col = lax.broadcasted_iota(jnp.int32, (LA_CHUNK, LA_CHUNK), 1)
    zeros = lambda n: jnp.zeros((n, HEAD_K), BF16)
    q_slabs, k_slabs = [], []
    for b0 in range(0, LA_CHUNK, LA_SUB):
        blk = slice(b0, b0 + LA_SUB)
        ref_row = g[b0 - 1:b0] if b0 else jnp.zeros((1, HEAD_K), F32)
        q_part = (qc[b0:] * jnp.exp(g[b0:] - ref_row)).astype(BF16)
        k_part = (kc[blk] * jnp.exp(ref_row - g[blk])).astype(BF16)
        q_slabs.append(jnp.concatenate([zeros(b0), q_part], axis=0) if b0 else q_part)
        k_slabs.append(jnp.concatenate(
            [z for z in (zeros(b0), k_part, zeros(LA_CHUNK - LA_SUB - b0)) if z.shape[0]], axis=0))
    att = _dot_nt(jnp.concatenate(q_slabs, axis=1), jnp.concatenate(k_slabs, axis=1))
    return jnp.where(col <= row, att, 0.0)


def _linear_attention_heads(qs, ks, vs, log_gs, st_ref, kbuf, gbuf, abuf, tb):
    n_heads, n_chunks = len(qs), tb // LA_CHUNK
    chunk = lambda a, c: a[c * LA_CHUNK:(c + 1) * LA_CHUNK]
    arows = lambda j, c: pl.ds((j * n_chunks + c) * LA_CHUNK, LA_CHUNK)
    gcs = [_chunk_cumsum(lg, tb) for lg in log_gs]
    steepest = functools.reduce(jnp.maximum, [-lg for lg in log_gs])
    safe = jnp.max(steepest) <= MAX_SAFE_STEP_DECAY

    @pl.when(safe)
    def _():
        for j in range(n_heads):
            for c in range(n_chunks):
                abuf[arows(j, c), :] = _att_factored(chunk(qs[j], c), chunk(ks[j], c), chunk(gcs[j], c))

    @pl.when(jnp.logical_not(safe))
    def _():
        for j in range(n_heads):
            kbuf[j, pl.ds(0, LA_SUB), :] = jnp.zeros((LA_SUB, HEAD_K), F32)
            gbuf[j, pl.ds(0, LA_SUB), :] = jnp.zeros((LA_SUB, HEAD_K), F32)
            kbuf[j, pl.ds(LA_SUB, tb), :] = ks[j]
            gbuf[j, pl.ds(LA_SUB, tb), :] = gcs[j]
            for c in range(n_chunks):
                abuf[arows(j, c), :] = _att_exact(
                    chunk(qs[j], c), chunk(gcs[j], c), kbuf.at[j], gbuf.at[j], c * LA_CHUNK)

    intra, updates, decays = {}, {}, {}
    for j in range(n_heads):
        for c in range(n_chunks):
            kc, vc, g = chunk(ks[j], c), chunk(vs[j], c), chunk(gcs[j], c)
            g_last = g[LA_CHUNK - 1:LA_CHUNK]
            intra[j, c] = _dot(abuf[arows(j, c), :].astype(BF16), vc)
            updates[j, c] = _dot_tn(vc, (kc * jnp.exp(g_last - g)).astype(BF16))
            decays[j, c] = jnp.exp(g_last)
    states = {}
    for j in range(n_heads):
        st = st_ref[j]
        for c in range(n_chunks):
            states[j, c] = st.astype(BF16)
            st = st * decays[j, c] + updates[j, c]
        st_ref[j] = st
    return [jnp.concatenate(
        [_dot_nt((chunk(qs[j], c) * jnp.exp(chunk(gcs[j], c))).astype(BF16), states[j, c]) + intra[j, c]
         for c in range(n_chunks)], axis=0) for j in range(n_heads)]


def _la_scratch(tb, dv, hp):
    return [pltpu.VMEM((hp, dv, HEAD_K), F32),
            pltpu.VMEM((hp, LA_SUB + tb, HEAD_K), F32),
            pltpu.VMEM((hp, LA_SUB + tb, HEAD_K), F32),
            pltpu.VMEM((hp * tb, LA_CHUNK), F32)]


def _head_cols(j, width):
    return slice(j * width, (j + 1) * width)


def _hgrn2_kernel(q_ref, f_ref, i_ref, gate_ref, lb_ref, on_ref, y_ref, st_ref, kbuf, gbuf, abuf, *, tb, hp):
    @pl.when(pl.program_id(1) == 0)
    def _():
        st_ref[...] = jnp.zeros_like(st_ref)

    qs, ks, vs, lgs = [], [], [], []
    for j in range(hp):
        cols = _head_cols(j, HEAD_K)
        lb = lb_ref[:, cols]
        f = lb + (1.0 - lb) * jax.nn.sigmoid(f_ref[:, cols].astype(F32))
        qs.append(q_ref[:, cols].astype(F32) * (HEAD_K ** -0.5))
        ks.append(1.0 - f)
        vs.append(i_ref[:, cols])
        lgs.append(jnp.log(f))
    outs = _linear_attention_heads(qs, ks, vs, lgs, st_ref, kbuf, gbuf, abuf, tb)
    for j in range(hp):
        cols = _head_cols(j, HEAD_V_C)
        y_ref[:, cols] = (_rms(outs[j], on_ref[:, cols]) * _silu(gate_ref[:, cols].astype(F32))).astype(BF16)


def _gla_kernel(q_ref, k_ref, v_ref, gate_ref, glr_ref, wup_ref, bg_ref, on_ref, y_ref,
                st_ref, kbuf, gbuf, abuf, *, tb, hp):
    @pl.when(pl.program_id(1) == 0)
    def _():
        st_ref[...] = jnp.zeros_like(st_ref)

    x = _dot(glr_ref[...], wup_ref[...]) + bg_ref[...]
    log_alpha = -(jnp.maximum(-x, 0.0) + jnp.log1p(jnp.exp(-jnp.abs(x)))) / GATE_LOGIT_NORM_D
    qs, ks, vs, lgs = [], [], [], []
    for j in range(hp):
        cols = _head_cols(j, HEAD_K)
        qs.append(q_ref[:, cols].astype(F32) * (HEAD_K ** -0.5))
        ks.append(k_ref[:, cols].astype(F32))
        vs.append(v_ref[:, _head_cols(j, HEAD_V_D)])
        lgs.append(log_alpha[:, cols])
    outs = _linear_attention_heads(qs, ks, vs, lgs, st_ref, kbuf, gbuf, abuf, tb)
    for j in range(hp):
        cols = _head_cols(j, HEAD_V_D)
        y_ref[:, cols] = (_rms(outs[j], on_ref[:, cols]) * _silu(gate_ref[:, cols].astype(F32))).astype(BF16)


_C_Q, _C_F, _C_I, _C_GATE = 0, 8, 16, 24
_D_Q, _D_K, _D_V, _D_GATE, _D_GLR = 32, 36, 40, 48, 56
PACKED_ODD = 57 * LANE


def _hgrn2(z, lb, onorm, tb, hp):
    s = z.shape[0]
    kw, vw = hp * HEAD_K, hp * HEAD_V_C
    col = lambda c0: pl.BlockSpec((tb, kw), lambda h, i: (i, c0 // hp + h))
    vec = lambda w: pl.BlockSpec((1, w), lambda h, i: (0, h))
    return pl.pallas_call(
        functools.partial(_hgrn2_kernel, tb=tb, hp=hp),
        grid=(N_HEADS_C // hp, s // tb),
        in_specs=[col(_C_Q), col(_C_F), col(_C_I), col(_C_GATE), vec(kw), vec(vw)],
        out_specs=pl.BlockSpec((tb, vw), lambda h, i: (i, h)),
        out_shape=jax.ShapeDtypeStruct((s, HALF), BF16),
        scratch_shapes=_la_scratch(tb, HEAD_V_C, hp),
        compiler_params=pltpu.CompilerParams(
            dimension_semantics=("arbitrary", "arbitrary"), vmem_limit_bytes=VMEM_LIMIT),
        name="hgrn2",
    )(z, z, z, z, lb.reshape(1, HALF), onorm.reshape(1, HALF))


def _gla(z, w_up, b_gate, onorm, tb, hp):
    s = z.shape[0]
    kw, vw = hp * HEAD_K, hp * HEAD_V_D
    col = lambda c0, w: pl.BlockSpec((tb, w), lambda h, i: (i, c0 * LANE // w + h))
    vec = lambda w: pl.BlockSpec((1, w), lambda h, i: (0, h))
    w_up = jnp.pad(w_up, ((0, LANE - GATE_RANK_D), (0, 0))).astype(BF16)
    return pl.pallas_call(
        functools.partial(_gla_kernel, tb=tb, hp=hp),
        grid=(N_HEADS_D // hp, s // tb),
        in_specs=[col(_D_Q, kw), col(_D_K, kw), col(_D_V, vw), col(_D_GATE, vw),
                  pl.BlockSpec((tb, LANE), lambda h, i: (i, _D_GLR)),
                  pl.BlockSpec((LANE, kw), lambda h, i: (0, h)),
                  vec(kw), vec(vw)],
        out_specs=pl.BlockSpec((tb, vw), lambda h, i: (i, h)),
        out_shape=jax.ShapeDtypeStruct((s, HALF), BF16),
        scratch_shapes=_la_scratch(tb, HEAD_V_D, hp),
        compiler_params=pltpu.CompilerParams(
            dimension_semantics=("arbitrary", "arbitrary"), vmem_limit_bytes=VMEM_LIMIT),
        name="gla",
    )(z, z, z, z, z, w_up, b_gate.reshape(1, N_HEADS_D * HEAD_K), onorm.reshape(1, HALF))


def _post_kernel(ya_ref, yb_ref, h_ref, p_ref, woa_ref, wob_ref, wple_ref, wg_ref,
                 pn_ref, gn_ref, fn_ref, o_ref, *, final):
    h1 = h_ref[...] + _dot(ya_ref[...], woa_ref[...]) + _dot(yb_ref[...], wob_ref[...])
    e = _rms(_dot(p_ref[...].astype(BF16), wple_ref[...]), pn_ref[...])
    gate = jax.nn.sigmoid(_dot(_rms(h1, gn_ref[...]).astype(BF16), wg_ref[...]))
    h2 = h1 + e * gate
    if final:
        h2 = _rms(h2, fn_ref[...])
    o_ref[...] = h2


def _post(ya, yb, ca, cb, h, p, layer, w_out, w_ple, w_gate, ple_norm, gate_norm, final_norm, tm, final):
    s, d = h.shape
    const = lambda shape, r=0: pl.BlockSpec(shape, lambda i: (r, 0), pipeline_mode=pl.Buffered(1))
    per_layer = lambda rows: pl.BlockSpec((None, rows, d), lambda i: (layer, 0, 0), pipeline_mode=pl.Buffered(1))
    return pl.pallas_call(
        functools.partial(_post_kernel, final=final),
        grid=(s // tm,),
        in_specs=[pl.BlockSpec((tm, HALF), lambda i: (i, ca)),
                  pl.BlockSpec((tm, HALF), lambda i: (i, cb)),
                  pl.BlockSpec((tm, d), lambda i: (i, 0)),
                  pl.BlockSpec((None, tm, PLE_DIM), lambda i: (layer, i, 0)),
                  const((HALF, d), 0), const((HALF, d), 1), per_layer(PLE_DIM), per_layer(d),
                  per_layer(1), per_layer(1), const((1, d))],
        out_specs=pl.BlockSpec((tm, d), lambda i: (i, 0)),
        out_shape=jax.ShapeDtypeStruct((s, d), F32),
        compiler_params=pltpu.CompilerParams(
            dimension_semantics=("arbitrary",), vmem_limit_bytes=VMEM_LIMIT),
        name="post_final" if final else "post",
    )(ya, yb, h, p, w_out, w_out, w_ple, w_gate, ple_norm, gate_norm, final_norm.reshape(1, d))


def _pack_odd_kernel(w_ref, o_ref):
    glr_start = _D_GATE * LANE
    rows = o_ref.shape[0]
    o_ref[:, :glr_start] = w_ref[:, :glr_start].astype(BF16)
    tail = w_ref[:, glr_start:]
    o_ref[:, glr_start:_D_GLR * LANE] = tail[:, GATE_RANK_D:].astype(BF16)
    o_ref[:, _D_GLR * LANE:] = jnp.concatenate(
        [tail[:, :GATE_RANK_D].astype(BF16), jnp.zeros((rows, LANE - GATE_RANK_D), BF16)], axis=1)


def _pack_odd(w, tr):
    d, n = w.shape
    return pl.pallas_call(
        _pack_odd_kernel,
        grid=(d // tr,),
        in_specs=[pl.BlockSpec((tr, n), lambda i: (i, 0))],
        out_specs=pl.BlockSpec((tr, PACKED_ODD), lambda i: (i, 0)),
        out_shape=jax.ShapeDtypeStruct((d, PACKED_ODD), BF16),
        compiler_params=pltpu.CompilerParams(
            dimension_semantics=("arbitrary",), vmem_limit_bytes=VMEM_LIMIT),
        name="pack_odd",
    )(w)


def kernel(x, p, norm_mix, w_in_even, sinks_a, vnorm_b, w_spatial_b, b_spatial_b, w_out_even,
           w_in_odd, lower_bounds_c, onorm_c, w_gate_up_d, b_gate_d, onorm_d, w_out_odd,
           w_ple_proj, ple_norm, ple_gate_norm, w_ple_gate, final_norm):
    assert x.shape[0] == 1 and x.shape[2] == D_MODEL and norm_mix.shape[0] == 2
    s = x.shape[1]
    h = x[0]
    p = p.reshape(2, s, PLE_DIM)
    w_ple, w_gate = w_ple_proj.astype(BF16), w_ple_gate.astype(BF16)
    ple_norm, gate_norm = ple_norm.reshape(2, 1, D_MODEL), ple_gate_norm.reshape(2, 1, D_MODEL)
    post = functools.partial(_post, p=p, w_ple=w_ple, w_gate=w_gate, ple_norm=ple_norm, gate_norm=gate_norm,
                             final_norm=final_norm, tm=256)

    z = _norm_matmul(h, norm_mix[0], w_in_even[0].astype(BF16), tm=min(s, 1024), tn=768)
    y = _mixer0(z, sinks_a[0], vnorm_b[0], w_spatial_b[0], b_spatial_b[0], tb=256)
    h = post(y, y, 0, 1, h, layer=0, w_out=w_out_even[0].astype(BF16), final=False)

    z = _norm_matmul(h, norm_mix[1], _pack_odd(w_in_odd[0], tr=256), tm=min(s, 512), tn=PACKED_ODD // 3)
    sm = jax.nn.softmax(lower_bounds_c.astype(F32), axis=0)
    lb = (jnp.cumsum(sm, axis=0) - sm[0])[1]
    yc = _hgrn2(z, lb, onorm_c[0], tb=256, hp=4)
    yd = _gla(z, w_gate_up_d[0], b_gate_d[0], onorm_d[0], tb=256, hp=2)
    out = post(yc, yd, 0, 0, h, layer=1, w_out=w_out_odd[0].astype(BF16), final=True)
    return out[None]
```

```python
import functools

import jax
import jax.numpy as jnp
from jax import lax
from jax.experimental import pallas as pl
from jax.experimental.pallas import tpu as pltpu

F32 = jnp.float32
BF16 = jnp.bfloat16

EPS = 1e-6
NEG_INF = -1e30

D_MODEL = 2048
PLE_DIM = 256
HALF = D_MODEL // 2
HEAD_A = 64
N_KV_A = 2
Q_PER_KV_A = 8
WINDOW = 128
N_GROUPS_B = 8
HEAD_K = 128
N_HEADS_C = 8
N_HEADS_D = 4
HEAD_V_C = 128
HEAD_V_D = 256
GATE_RANK_D = 16
GATE_LOGIT_NORM_D = 16.0
LA_CHUNK = 64
LA_SUB = 16
MAX_SAFE_STEP_DECAY = 60.0 / LA_SUB
LANE = 128
VMEM_LIMIT = 56 * 1024 * 1024


def _dot(a, b):
    return jnp.dot(a, b, preferred_element_type=F32)


def _dot_nt(a, b):
    return lax.dot_general(a, b, (((1,), (1,)), ((), ())), preferred_element_type=F32)


def _dot_tn(a, b):
    return lax.dot_general(a, b, (((0,), (0,)), ((), ())), preferred_element_type=F32)


def _rms(x, gain):
    return x * lax.rsqrt(jnp.mean(x * x, axis=-1, keepdims=True) + EPS) * gain


def _silu(x):
    return x * jax.nn.sigmoid(x)


def _gelu(x):
    return 0.5 * x * (1.0 + lax.erf(x * (2.0 ** -0.5)))


def _norm_matmul_kernel(x_ref, g_ref, w_ref, o_ref, xn_ref):
    @pl.when(pl.program_id(1) == 0)
    def _():
        xn_ref[...] = _rms(x_ref[...], g_ref[...]).astype(BF16)

    o_ref[...] = _dot(xn_ref[...], w_ref[...]).astype(o_ref.dtype)


def _matmul_kernel(x_ref, w_ref, o_ref):
    o_ref[...] = _dot(x_ref[...], w_ref[...]).astype(o_ref.dtype)


def _norm_matmul(x, gain, w, tm, tn):
    s, d = x.shape
    n = w.shape[1]
    x_spec = pl.BlockSpec((tm, d), lambda i, j: (i, 0))
    w_spec = pl.BlockSpec((d, tn), lambda i, j: (0, j))
    common = dict(
        grid=(s // tm, n // tn),
        out_specs=pl.BlockSpec((tm, tn), lambda i, j: (i, j)),
        out_shape=jax.ShapeDtypeStruct((s, n), BF16),
        compiler_params=pltpu.CompilerParams(
            dimension_semantics=("arbitrary", "arbitrary"), vmem_limit_bytes=VMEM_LIMIT))
    if gain is None:
        return pl.pallas_call(_matmul_kernel, in_specs=[x_spec, w_spec], name="matmul", **common)(x, w)
    return pl.pallas_call(
        _norm_matmul_kernel,
        in_specs=[x_spec, pl.BlockSpec((1, d), lambda i, j: (0, 0)), w_spec],
        scratch_shapes=[pltpu.VMEM((tm, d), BF16)],
        name="norm_matmul", **common)(x, gain.reshape(1, d), w)


_A_Q, _A_K, _A_V, _A_GATE, _B_U, _B_V, _B_GATE = 0, 1024, 1152, 1280, 2304, 3328, 4352
IN_EVEN = 5376


def _mixer0_kernel(sinks_ref, z_ref, kp_ref, vp_ref, vn_ref, wsp_ref, bsp_ref, y_ref, *, tb):
    first = pl.program_id(0) == 0
    lane = lax.broadcasted_iota(jnp.int32, (1, LANE), 1)
    lo = lane < HEAD_A
    qi = lax.broadcasted_iota(jnp.int32, (WINDOW, 2 * WINDOW), 0)
    kj = lax.broadcasted_iota(jnp.int32, (WINDOW, 2 * WINDOW), 1)
    band = (kj > qi) & (kj <= qi + WINDOW)
    tri = (lax.broadcasted_iota(jnp.int32, (WINDOW, WINDOW), 0)
           >= lax.broadcasted_iota(jnp.int32, (WINDOW, WINDOW), 1))
    zero_bf = jnp.zeros((), BF16)
    n_q = Q_PER_KV_A

    def swap_halves(t):
        return jnp.concatenate([t[:, HEAD_A:], t[:, :HEAD_A]], axis=1)

    def both_halves(t, t_sw, h):
        return jnp.where(lo, t, t_sw) if h == 0 else jnp.where(lo, t_sw, t)

    for r in range(tb // WINDOW):
        rows = pl.ds(r * WINDOW, WINDOW)
        zcols = lambda c0, w: z_ref[rows, pl.ds(c0, w)]
        if r == 0:
            k_prev, v_prev = kp_ref[...], vp_ref[...]
            mask = band & ((kj >= WINDOW) | jnp.logical_not(first))
        else:
            prev = pl.ds((r - 1) * WINDOW, WINDOW)
            k_prev, v_prev = z_ref[prev, pl.ds(_A_K, LANE)], z_ref[prev, pl.ds(_A_V, LANE)]
            mask = band
        kk = jnp.concatenate([k_prev, zcols(_A_K, LANE)], axis=0)
        vv = jnp.concatenate([v_prev, zcols(_A_V, LANE)], axis=0)
        kk_sw, vv_sw = swap_halves(kk), swap_halves(vv)

        scores = []
        for h in range(N_KV_A):
            parts = []
            for j in range(n_q // 2):
                qp = zcols(_A_Q + (h * n_q + 2 * j) * HEAD_A, LANE) * jnp.asarray(HEAD_A ** -0.5, BF16)
                parts += [jnp.where(lo, qp, zero_bf), jnp.where(lo, zero_bf, qp)]
            scores.append(_dot_nt(jnp.concatenate(parts, axis=0), both_halves(kk, kk_sw, h)))

        u = _gelu(zcols(_B_U, HALF).astype(F32))
        vg = _rms(_gelu(zcols(_B_V, HALF).astype(F32)), vn_ref[...]).astype(BF16)
        gb = zcols(_B_GATE, HALF).astype(F32)
        for g in range(N_GROUPS_B):
            cols = slice(g * LANE, (g + 1) * LANE)
            w = jnp.where(tri, wsp_ref[g], 0.0).astype(BF16)
            mixed = _dot(w, vg[:, cols]) + bsp_ref[:, g:g + 1]
            y_ref[rows, pl.ds(HALF + g * LANE, LANE)] = (u[:, cols] * mixed * _silu(gb[:, cols])).astype(BF16)

        for h in range(N_KV_A):
            ps, rs = [], []
            for i in range(n_q):
                sink = sinks_ref[h * n_q + i]
                sc = jnp.where(mask, scores[h][i * WINDOW:(i + 1) * WINDOW], NEG_INF)
                m = jnp.maximum(jnp.max(sc, axis=-1, keepdims=True), sink)
                p = jnp.exp(sc - m)
                rs.append(1.0 / (jnp.sum(p, axis=-1, keepdims=True) + jnp.exp(sink - m)))
                ps.append(p.astype(BF16))
            v_both = both_halves(vv, vv_sw, h)
            v2 = jnp.concatenate([jnp.where(lo, v_both, zero_bf), jnp.where(lo, zero_bf, v_both)], axis=0)
            p_pairs = jnp.concatenate(
                [jnp.concatenate(ps[2 * j:2 * j + 2], axis=1) for j in range(n_q // 2)], axis=0)
            o = _dot(p_pairs, v2)
            for j in range(n_q // 2):
                c0 = (h * n_q + 2 * j) * HEAD_A
                oj = o[j * WINDOW:(j + 1) * WINDOW] * jnp.where(lo, rs[2 * j], rs[2 * j + 1])
                g = zcols(_A_GATE + c0, LANE).astype(F32)
                y_ref[rows, pl.ds(c0, LANE)] = (oj * _silu(g)).astype(BF16)


def _mixer0(z, sinks, vnorm, w_sp, b_sp, tb):
    s = z.shape[0]
    nb = tb // WINDOW
    prv = lambda c0: pl.BlockSpec((WINDOW, LANE), lambda i: (jnp.maximum(i * nb - 1, 0), c0 // LANE))
    full = lambda a: pl.BlockSpec(a.shape, lambda i: (0,) * a.ndim)
    vnorm = vnorm.reshape(1, HALF)
    b_t = b_sp.T
    return pl.pallas_call(
        functools.partial(_mixer0_kernel, tb=tb),
        grid=(s // tb,),
        in_specs=[pl.BlockSpec(memory_space=pltpu.SMEM),
                  pl.BlockSpec((tb, IN_EVEN), lambda i: (i, 0)), prv(_A_K), prv(_A_V),
                  full(vnorm), full(w_sp), full(b_t)],
        out_specs=pl.BlockSpec((tb, D_MODEL), lambda i: (i, 0)),
        out_shape=jax.ShapeDtypeStruct((s, D_MODEL), BF16),
        compiler_params=pltpu.CompilerParams(
            dimension_semantics=("arbitrary",), vmem_limit_bytes=VMEM_LIMIT),
        name="mixer0",
    )(sinks, z, z, z, vnorm, w_sp, b_t)


def _chunk_cumsum(log_g, tb):
    t = lax.broadcasted_iota(jnp.int32, (tb, tb), 0)
    s = lax.broadcasted_iota(jnp.int32, (tb, tb), 1)
    tri = jnp.where((s <= t) & (t // LA_CHUNK == s // LA_CHUNK), 1.0, 0.0).astype(BF16)
    hi = log_g.astype(BF16)
    low = (log_g - hi.astype(F32)).astype(BF16)
    return _dot(tri, hi) + _dot(tri, low)


def _att_exact(qc, g, kbuf, gbuf, r0):
    row = lax.broadcasted_iota(jnp.int32, (LA_CHUNK, LA_CHUNK), 0)
    col = lax.broadcasted_iota(jnp.int32, (LA_CHUNK, LA_CHUNK), 1)
    sub_pos = lax.broadcasted_iota(jnp.int32, (LA_CHUNK, 1), 0) % LA_SUB
    zeros = lambda n: jnp.zeros((n, HEAD_K), F32)
    kc = kbuf[pl.ds(LA_SUB + r0, LA_CHUNK), :]
    att = jnp.zeros((LA_CHUNK, LA_CHUNK), F32)
    for d in range(LA_SUB):
        ks = kbuf[pl.ds(LA_SUB + r0 - d, LA_CHUNK), :]
        gs = gbuf[pl.ds(LA_SUB + r0 - d, LA_CHUNK), :]
        pd = jnp.where(sub_pos >= d, qc * ks * jnp.exp(g - gs), 0.0)
        att = jnp.where(col == row - d, jnp.sum(pd, axis=-1, keepdims=True), att)
    m = LA_SUB
    while m < LA_CHUNK:
        q_parts, k_parts = [], []
        for b0 in range(0, LA_CHUNK, 2 * m):
            ref_row = g[b0 + m - 1:b0 + m]
            lower, upper = slice(b0, b0 + m), slice(b0 + m, b0 + 2 * m)
            q_parts += [zeros(m), qc[upper] * jnp.exp(g[upper] - ref_row)]
            k_parts += [kc[lower] * jnp.exp(ref_row - g[lower]), zeros(m)]
        part = _dot_nt(jnp.concatenate(q_parts, axis=0).astype(BF16),
                       jnp.concatenate(k_parts, axis=0).astype(BF16))
        if 2 * m < LA_CHUNK:
            part = jnp.where(row // (2 * m) == col // (2 * m), part, 0.0)
        att = att + part
        m *= 2
    return att


def _att_factored(qc, kc, g):
    row = lax.broadcasted_iota(jnp.int32, (LA_CHUNK, LA_CHUNK), 0)
    col = lax.broadcasted_iota(jnp.int32, (LA_CHUNK, LA_CHUNK), 1)
    zeros = lambda n: jnp.zeros((n, HEAD_K), BF16)
    q_slabs, k_slabs = [], []
    for b0 in range(0, LA_CHUNK, LA_SUB):
        blk = slice(b0, b0 + LA_SUB)
        ref_row = g[b0 - 1:b0] if b0 else jnp.zeros((1, HEAD_K), F32)
        q_part = (qc[b0:] * jnp.exp(g[b0:] - ref_row)).astype(BF16)
        k_part = (kc[blk] * jnp.exp(ref_row - g[blk])).astype(BF16)
        q_slabs.append(jnp.concatenate([zeros(b0), q_part], axis=0) if b0 else q_part)
        k_slabs.append(jnp.concatenate(
            [z for z in (zeros(b0), k_part, zeros(LA_CHUNK - LA_SUB - b0)) if z.shape[0]], axis=0))
    att = _dot_nt(jnp.concatenate(q_slabs, axis=1), jnp.concatenate(k_slabs, axis=1))
    return jnp.where(col <= row, att, 0.0)


def _linear_attention_heads(qs, ks, vs, log_gs, st_ref, kbuf, gbuf, abuf, tb):
    n_heads, n_chunks = len(qs), tb // LA_CHUNK
    chunk = lambda a, c: a[c * LA_CHUNK:(c + 1) * LA_CHUNK]
    arows = lambda j, c: pl.ds((j * n_chunks + c) * LA_CHUNK, LA_CHUNK)
    gcs = [_chunk_cumsum(lg, tb) for lg in log_gs]
    steepest = functools.reduce(jnp.maximum, [-lg for lg in log_gs])
    safe = jnp.max(steepest) <= MAX_SAFE_STEP_DECAY

    @pl.when(safe)
    def _():
        for j in range(n_heads):
            for c in range(n_chunks):
                abuf[arows(j, c), :] = _att_factored(chunk(qs[j], c), chunk(ks[j], c), chunk(gcs[j], c))

    @pl.when(jnp.logical_not(safe))
    def _():
        for j in range(n_heads):
            kbuf[j, pl.ds(0, LA_SUB), :] = jnp.zeros((LA_SUB, HEAD_K), F32)
            gbuf[j, pl.ds(0, LA_SUB), :] = jnp.zeros((LA_SUB, HEAD_K), F32)
            kbuf[j, pl.ds(LA_SUB, tb), :] = ks[j]
            gbuf[j, pl.ds(LA_SUB, tb), :] = gcs[j]
            for c in range(n_chunks):
                abuf[arows(j, c), :] = _att_exact(
                    chunk(qs[j], c), chunk(gcs[j], c), kbuf.at[j], gbuf.at[j], c * LA_CHUNK)

    intra, updates, decays = {}, {}, {}
    for j in range(n_heads):
        for c in range(n_chunks):
            kc, vc, g = chunk(ks[j], c), chunk(vs[j], c), chunk(gcs[j], c)
            g_last = g[LA_CHUNK - 1:LA_CHUNK]
            intra[j, c] = _dot(abuf[arows(j, c), :].astype(BF16), vc)
            updates[j, c] = _dot_tn(vc, (kc * jnp.exp(g_last - g)).astype(BF16))
            decays[j, c] = jnp.exp(g_last)
    states = {}
    for j in range(n_heads):
        st = st_ref[j]
        for c in range(n_chunks):
            states[j, c] = st.astype(BF16)
            st = st * decays[j, c] + updates[j, c]
        st_ref[j] = st
    return [jnp.concatenate(
        [_dot_nt((chunk(qs[j], c) * jnp.exp(chunk(gcs[j], c))).astype(BF16), states[j, c]) + intra[j, c]
         for c in range(n_chunks)], axis=0) for j in range(n_heads)]


def _la_scratch(tb, dv, hp):
    return [pltpu.VMEM((hp, dv, HEAD_K), F32),
            pltpu.VMEM((hp, LA_SUB + tb, HEAD_K), F32),
            pltpu.VMEM((hp, LA_SUB + tb, HEAD_K), F32),
            pltpu.VMEM((hp * tb, LA_CHUNK), F32)]


def _head_cols(j, width):
    return slice(j * width, (j + 1) * width)


def _hgrn2_kernel(q_ref, f_ref, i_ref, gate_ref, lb_ref, on_ref, y_ref, st_ref, kbuf, gbuf, abuf, *, tb, hp):
    @pl.when(pl.program_id(1) == 0)
    def _():
        st_ref[...] = jnp.zeros_like(st_ref)

    qs, ks, vs, lgs = [], [], [], []
    for j in range(hp):
        cols = _head_cols(j, HEAD_K)
        lb = lb_ref[:, cols]
        f = lb + (1.0 - lb) * jax.nn.sigmoid(f_ref[:, cols].astype(F32))
        qs.append(q_ref[:, cols].astype(F32) * (HEAD_K ** -0.5))
        ks.append(1.0 - f)
        vs.append(i_ref[:, cols])
        lgs.append(jnp.log(f))
    outs = _linear_attention_heads(qs, ks, vs, lgs, st_ref, kbuf, gbuf, abuf, tb)
    for j in range(hp):
        cols = _head_cols(j, HEAD_V_C)
        y_ref[:, cols] = (_rms(outs[j], on_ref[:, cols]) * _silu(gate_ref[:, cols].astype(F32))).astype(BF16)


def _gla_kernel(q_ref, k_ref, v_ref, gate_ref, glr_ref, wup_ref, bg_ref, on_ref, y_ref,
                st_ref, kbuf, gbuf, abuf, *, tb, hp):
    @pl.when(pl.program_id(1) == 0)
    def _():
        st_ref[...] = jnp.zeros_like(st_ref)

    x = _dot(glr_ref[...], wup_ref[...]) + bg_ref[...]
    log_alpha = -(jnp.maximum(-x, 0.0) + jnp.log1p(jnp.exp(-jnp.abs(x)))) / GATE_LOGIT_NORM_D
    qs, ks, vs, lgs = [], [], [], []
    for j in range(hp):
        cols = _head_cols(j, HEAD_K)
        qs.append(q_ref[:, cols].astype(F32) * (HEAD_K ** -0.5))
        ks.append(k_ref[:, cols].astype(F32))
        vs.append(v_ref[:, _head_cols(j, HEAD_V_D)])
        lgs.append(log_alpha[:, cols])
    outs = _linear_attention_heads(qs, ks, vs, lgs, st_ref, kbuf, gbuf, abuf, tb)
    for j in range(hp):
        cols = _head_cols(j, HEAD_V_D)
        y_ref[:, cols] = (_rms(outs[j], on_ref[:, cols]) * _silu(gate_ref[:, cols].astype(F32))).astype(BF16)


_C_Q, _C_F, _C_I, _C_GATE = 0, 8, 16, 24
_D_Q, _D_K, _D_V, _D_GATE, _D_GLR = 32, 36, 40, 48, 56
PACKED_ODD = 57 * LANE


def _hgrn2(z, lb, onorm, tb, hp):
    s = z.shape[0]
    kw, vw = hp * HEAD_K, hp * HEAD_V_C
    col = lambda c0: pl.BlockSpec((tb, kw), lambda h, i: (i, c0 // hp + h))
    vec = lambda w: pl.BlockSpec((1, w), lambda h, i: (0, h))
    return pl.pallas_call(
        functools.partial(_hgrn2_kernel, tb=tb, hp=hp),
        grid=(N_HEADS_C // hp, s // tb),
        in_specs=[col(_C_Q), col(_C_F), col(_C_I), col(_C_GATE), vec(kw), vec(vw)],
        out_specs=pl.BlockSpec((tb, vw), lambda h, i: (i, h)),
        out_shape=jax.ShapeDtypeStruct((s, HALF), BF16),
        scratch_shapes=_la_scratch(tb, HEAD_V_C, hp),
        compiler_params=pltpu.CompilerParams(
            dimension_semantics=("arbitrary", "arbitrary"), vmem_limit_bytes=VMEM_LIMIT),
        name="hgrn2",
    )(z, z, z, z, lb.reshape(1, HALF), onorm.reshape(1, HALF))


def _gla(z, w_up, b_gate, onorm, tb, hp):
    s = z.shape[0]
    kw, vw = hp * HEAD_K, hp * HEAD_V_D
    col = lambda c0, w: pl.BlockSpec((tb, w), lambda h, i: (i, c0 * LANE // w + h))
    vec = lambda w: pl.BlockSpec((1, w), lambda h, i: (0, h))
    w_up = jnp.pad(w_up, ((0, LANE - GATE_RANK_D), (0, 0))).astype(BF16)
    return pl.pallas_call(
        functools.partial(_gla_kernel, tb=tb, hp=hp),
        grid=(N_HEADS_D // hp, s // tb),
        in_specs=[col(_D_Q, kw), col(_D_K, kw), col(_D_V, vw), col(_D_GATE, vw),
                  pl.BlockSpec((tb, LANE), lambda h, i: (i, _D_GLR)),
                  pl.BlockSpec((LANE, kw), lambda h, i: (0, h)),
                  vec(kw), vec(vw)],
        out_specs=pl.BlockSpec((tb, vw), lambda h, i: (i, h)),
        out_shape=jax.ShapeDtypeStruct((s, HALF), BF16),
        scratch_shapes=_la_scratch(tb, HEAD_V_D, hp),
        compiler_params=pltpu.CompilerParams(
            dimension_semantics=("arbitrary", "arbitrary"), vmem_limit_bytes=VMEM_LIMIT),
        name="gla",
    )(z, z, z, z, z, w_up, b_gate.reshape(1, N_HEADS_D * HEAD_K), onorm.reshape(1, HALF))


def _post_kernel(ya_ref, yb_ref, h_ref, p_ref, woa_ref, wob_ref, wple_ref, wg_ref,
                 pn_ref, gn_ref, nn_ref, *out_refs, final):
    h1 = h_ref[...] + _dot(ya_ref[...], woa_ref[...]) + _dot(yb_ref[...], wob_ref[...])
    e = _rms(_dot(p_ref[...].astype(BF16), wple_ref[...]), pn_ref[...])
    gate = jax.nn.sigmoid(_dot(_rms(h1, gn_ref[...]).astype(BF16), wg_ref[...]))
    h2 = h1 + e * gate
    normed = _rms(h2, nn_ref[...])
    if final:
        out_refs[0][...] = normed
    else:
        out_refs[0][...] = h2
        out_refs[1][...] = normed.astype(BF16)


def _post(ya, yb, ca, cb, h, p, layer, w_out, w_ple, w_gate, ple_norm, gate_norm, next_norm, tm, final):
    s, d = h.shape
    const = lambda shape, r=0: pl.BlockSpec(shape, lambda i: (r, 0), pipeline_mode=pl.Buffered(1))
    per_layer = lambda rows: pl.BlockSpec((None, rows, d), lambda i: (layer, 0, 0), pipeline_mode=pl.Buffered(1))
    row_block = pl.BlockSpec((tm, d), lambda i: (i, 0))
    out_f32, out_bf16 = jax.ShapeDtypeStruct((s, d), F32), jax.ShapeDtypeStruct((s, d), BF16)
    return pl.pallas_call(
        functools.partial(_post_kernel, final=final),
        grid=(s // tm,),
        in_specs=[pl.BlockSpec((tm, HALF), lambda i: (i, ca)),
                  pl.BlockSpec((tm, HALF), lambda i: (i, cb)),
                  row_block,
                  pl.BlockSpec((None, tm, PLE_DIM), lambda i: (layer, i, 0)),
                  const((HALF, d), 0), const((HALF, d), 1), per_layer(PLE_DIM), per_layer(d),
                  per_layer(1), per_layer(1), const((1, d))],
        out_specs=row_block if final else (row_block, row_block),
        out_shape=out_f32 if final else (out_f32, out_bf16),
        compiler_params=pltpu.CompilerParams(
            dimension_semantics=("arbitrary",), vmem_limit_bytes=VMEM_LIMIT),
        name="post_final" if final else "post",
    )(ya, yb, h, p, w_out, w_out, w_ple, w_gate, ple_norm, gate_norm, next_norm.reshape(1, d))


def _pack_odd_kernel(w_ref, o_ref):
    glr_start = _D_GATE * LANE
    rows = o_ref.shape[0]
    o_ref[:, :glr_start] = w_ref[:, :glr_start].astype(BF16)
    tail = w_ref[:, glr_start:]
    o_ref[:, glr_start:_D_GLR * LANE] = tail[:, GATE_RANK_D:].astype(BF16)
    o_ref[:, _D_GLR * LANE:] = jnp.concatenate(
        [tail[:, :GATE_RANK_D].astype(BF16), jnp.zeros((rows, LANE - GATE_RANK_D), BF16)], axis=1)


def _pack_odd(w, tr):
    d, n = w.shape
    return pl.pallas_call(
        _pack_odd_kernel,
        grid=(d // tr,),
        in_specs=[pl.BlockSpec((tr, n), lambda i: (i, 0))],
        out_specs=pl.BlockSpec((tr, PACKED_ODD), lambda i: (i, 0)),
        out_shape=jax.ShapeDtypeStruct((d, PACKED_ODD), BF16),
        compiler_params=pltpu.CompilerParams(
            dimension_semantics=("arbitrary",), vmem_limit_bytes=VMEM_LIMIT),
        name="pack_odd",
    )(w)


def kernel(x, p, norm_mix, w_in_even, sinks_a, vnorm_b, w_spatial_b, b_spatial_b, w_out_even,
           w_in_odd, lower_bounds_c, onorm_c, w_gate_up_d, b_gate_d, onorm_d, w_out_odd,
           w_ple_proj, ple_norm, ple_gate_norm, w_ple_gate, final_norm):
    assert x.shape[0] == 1 and x.shape[2] == D_MODEL and norm_mix.shape[0] == 2
    s = x.shape[1]
    h = x[0]
    p = p.reshape(2, s, PLE_DIM)
    w_ple, w_gate = w_ple_proj.astype(BF16), w_ple_gate.astype(BF16)
    ple_norm, gate_norm = ple_norm.reshape(2, 1, D_MODEL), ple_gate_norm.reshape(2, 1, D_MODEL)
    post = functools.partial(_post, p=p, w_ple=w_ple, w_gate=w_gate, ple_norm=ple_norm, gate_norm=gate_norm,
                             tm=256)

    z = _norm_matmul(h, norm_mix[0], w_in_even[0].astype(BF16), tm=min(s, 1024), tn=IN_EVEN // 3)
    y = _mixer0(z, sinks_a[0], vnorm_b[0], w_spatial_b[0], b_spatial_b[0], tb=256)
    h, hn = post(y, y, 0, 1, h, layer=0, w_out=w_out_even[0].astype(BF16), next_norm=norm_mix[1], final=False)

    z = _norm_matmul(hn, None, _pack_odd(w_in_odd[0], tr=256), tm=min(s, 1024), tn=PACKED_ODD // 3)
    sm = jax.nn.softmax(lower_bounds_c.astype(F32), axis=0)
    lb = (jnp.cumsum(sm, axis=0) - sm[0])[1]
    yc = _hgrn2(z, lb, onorm_c[0], tb=256, hp=4)
    yd = _gla(z, w_gate_up_d[0], b_gate_d[0], onorm_d[0], tb=256, hp=2)
    out = post(yc, yd, 0, 0, h, layer=1, w_out=w_out_odd[0].astype(BF16), next_norm=final_norm, final=True)
    return out[None]
```

```python
import functools

import jax
import jax.numpy as jnp
from jax import lax
from jax.experimental import pallas as pl
from jax.experimental.pallas import tpu as pltpu

F32 = jnp.float32
BF16 = jnp.bfloat16

EPS = 1e-6
NEG_INF = -1e30

D_MODEL = 2048
PLE_DIM = 256
HALF = D_MODEL // 2
HEAD_A = 64
N_KV_A = 2
Q_PER_KV_A = 8
WINDOW = 128
N_GROUPS_B = 8
HEAD_K = 128
N_HEADS_C = 8
N_HEADS_D = 4
HEAD_V_C = 128
HEAD_V_D = 256
GATE_RANK_D = 16
GATE_LOGIT_NORM_D = 16.0
LA_CHUNK = 64
LA_SUB = 16
MAX_SAFE_STEP_DECAY = 60.0 / LA_SUB
LANE = 128
VMEM_LIMIT = 56 * 1024 * 1024


def _dot(a, b):
    return jnp.dot(a, b, preferred_element_type=F32)


def _dot_nt(a, b):
    return lax.dot_general(a, b, (((1,), (1,)), ((), ())), preferred_element_type=F32)


def _dot_tn(a, b):
    return lax.dot_general(a, b, (((0,), (0,)), ((), ())), preferred_element_type=F32)


def _rms(x, gain):
    return x * lax.rsqrt(jnp.mean(x * x, axis=-1, keepdims=True) + EPS) * gain


def _silu(x):
    return x * jax.nn.sigmoid(x)


def _gelu(x):
    return 0.5 * x * (1.0 + lax.erf(x * (2.0 ** -0.5)))


def _norm_matmul_kernel(x_ref, g_ref, w_ref, o_ref, xn_ref):
    @pl.when(pl.program_id(1) == 0)
    def _():
        xn_ref[...] = _rms(x_ref[...], g_ref[...]).astype(BF16)

    o_ref[...] = _dot(xn_ref[...], w_ref[...]).astype(o_ref.dtype)


def _matmul_kernel(x_ref, w_ref, o_ref):
    o_ref[...] = _dot(x_ref[...], w_ref[...]).astype(o_ref.dtype)


def _norm_matmul(x, gain, w, tm, tn):
    s, d = x.shape
    n = w.shape[1]
    x_spec = pl.BlockSpec((tm, d), lambda i, j: (i, 0))
    w_spec = pl.BlockSpec((d, tn), lambda i, j: (0, j))
    common = dict(
        grid=(s // tm, n // tn),
        out_specs=pl.BlockSpec((tm, tn), lambda i, j: (i, j)),
        out_shape=jax.ShapeDtypeStruct((s, n), BF16),
        compiler_params=pltpu.CompilerParams(
            dimension_semantics=("arbitrary", "arbitrary"), vmem_limit_bytes=VMEM_LIMIT))
    if gain is None:
        return pl.pallas_call(_matmul_kernel, in_specs=[x_spec, w_spec], name="matmul", **common)(x, w)
    return pl.pallas_call(
        _norm_matmul_kernel,
        in_specs=[x_spec, pl.BlockSpec((1, d), lambda i, j: (0, 0)), w_spec],
        scratch_shapes=[pltpu.VMEM((tm, d), BF16)],
        name="norm_matmul", **common)(x, gain.reshape(1, d), w)


_A_Q, _A_K, _A_V, _A_GATE, _B_U, _B_V, _B_GATE = 0, 1024, 1152, 1280, 2304, 3328, 4352
IN_EVEN = 5376


def _mixer0_phases(sinks_ref, z_ref, kp_ref, vp_ref, vn_ref, wsp_ref, bsp_ref, y_ref, first, r):
    lane = lax.broadcasted_iota(jnp.int32, (1, LANE), 1)
    lo = lane < HEAD_A
    qi = lax.broadcasted_iota(jnp.int32, (WINDOW, 2 * WINDOW), 0)
    kj = lax.broadcasted_iota(jnp.int32, (WINDOW, 2 * WINDOW), 1)
    band = (kj > qi) & (kj <= qi + WINDOW)
    tri = (lax.broadcasted_iota(jnp.int32, (WINDOW, WINDOW), 0)
           >= lax.broadcasted_iota(jnp.int32, (WINDOW, WINDOW), 1))
    zero_bf = jnp.zeros((), BF16)
    n_q = Q_PER_KV_A
    rows = pl.ds(r * WINDOW, WINDOW)
    zcols = lambda c0, w: z_ref[rows, pl.ds(c0, w)]
    st = {}

    def swap_halves(t):
        return jnp.concatenate([t[:, HEAD_A:], t[:, :HEAD_A]], axis=1)

    def both_halves(t, t_sw, h):
        return jnp.where(lo, t, t_sw) if h == 0 else jnp.where(lo, t_sw, t)

    def scores():
        if r == 0:
            k_prev, v_prev = kp_ref[...], vp_ref[...]
            st["mask"] = band & ((kj >= WINDOW) | jnp.logical_not(first))
        else:
            prev = pl.ds((r - 1) * WINDOW, WINDOW)
            k_prev, v_prev = z_ref[prev, pl.ds(_A_K, LANE)], z_ref[prev, pl.ds(_A_V, LANE)]
            st["mask"] = band
        kk = jnp.concatenate([k_prev, zcols(_A_K, LANE)], axis=0)
        vv = jnp.concatenate([v_prev, zcols(_A_V, LANE)], axis=0)
        kk_sw = swap_halves(kk)
        st["vv"], st["vv_sw"] = vv, swap_halves(vv)
        st["scores"] = []
        for h in range(N_KV_A):
            parts = []
            for j in range(n_q // 2):
                qp = zcols(_A_Q + (h * n_q + 2 * j) * HEAD_A, LANE) * jnp.asarray(HEAD_A ** -0.5, BF16)
                parts += [jnp.where(lo, qp, zero_bf), jnp.where(lo, zero_bf, qp)]
            st["scores"].append(_dot_nt(jnp.concatenate(parts, axis=0), both_halves(kk, kk_sw, h)))

    def gmlp():
        u = _gelu(zcols(_B_U, HALF).astype(F32))
        vg = _rms(_gelu(zcols(_B_V, HALF).astype(F32)), vn_ref[...]).astype(BF16)
        gb = zcols(_B_GATE, HALF).astype(F32)
        for g in range(N_GROUPS_B):
            cols = slice(g * LANE, (g + 1) * LANE)
            w = jnp.where(tri, wsp_ref[g], 0.0).astype(BF16)
            mixed = _dot(w, vg[:, cols]) + bsp_ref[:, g:g + 1]
            y_ref[rows, pl.ds(HALF + g * LANE, LANE)] = (u[:, cols] * mixed * _silu(gb[:, cols])).astype(BF16)

    def softmax_pv():
        mask, vv, vv_sw = st["mask"], st["vv"], st["vv_sw"]
        for h in range(N_KV_A):
            ps, rs = [], []
            for i in range(n_q):
                sink = sinks_ref[h * n_q + i]
                sc = jnp.where(mask, st["scores"][h][i * WINDOW:(i + 1) * WINDOW], NEG_INF)
                m = jnp.maximum(jnp.max(sc, axis=-1, keepdims=True), sink)
                p = jnp.exp(sc - m)
                rs.append(1.0 / (jnp.sum(p, axis=-1, keepdims=True) + jnp.exp(sink - m)))
                ps.append(p.astype(BF16))
            v_both = both_halves(vv, vv_sw, h)
            v2 = jnp.concatenate([jnp.where(lo, v_both, zero_bf), jnp.where(lo, zero_bf, v_both)], axis=0)
            p_pairs = jnp.concatenate(
                [jnp.concatenate(ps[2 * j:2 * j + 2], axis=1) for j in range(n_q // 2)], axis=0)
            o = _dot(p_pairs, v2)
            for j in range(n_q // 2):
                c0 = (h * n_q + 2 * j) * HEAD_A
                oj = o[j * WINDOW:(j + 1) * WINDOW] * jnp.where(lo, rs[2 * j], rs[2 * j + 1])
                g = zcols(_A_GATE + c0, LANE).astype(F32)
                y_ref[rows, pl.ds(c0, LANE)] = (oj * _silu(g)).astype(BF16)

    return scores, gmlp, softmax_pv


POST_SPLIT = 4


def _post_phases(ya, yb, h_ref, p_ref, woa_ref, wob_ref, wple_ref, wg_ref, pn_ref, gn_ref, nn_ref, out_refs, final):
    cw = D_MODEL // POST_SPLIT
    st = {"h1": [], "gate": []}

    def out_proj(c):
        cols = pl.ds(c * cw, cw)
        st["h1"].append(h_ref[:, cols] + _dot(ya(), woa_ref[:, cols]) + _dot(yb(), wob_ref[:, cols]))

    def embed():
        st["e"] = _rms(_dot(p_ref[...].astype(BF16), wple_ref[...]), pn_ref[...])

    def norm():
        st["h1"] = jnp.concatenate(st["h1"], axis=1)
        st["hn1"] = _rms(st["h1"], gn_ref[...]).astype(BF16)

    def gate(c):
        st["gate"].append(jax.nn.sigmoid(_dot(st["hn1"], wg_ref[:, pl.ds(c * cw, cw)])))

    def finish():
        h2 = st["h1"] + st["e"] * jnp.concatenate(st["gate"], axis=1)
        normed = _rms(h2, nn_ref[...])
        if final:
            out_refs[0][...] = normed
        else:
            out_refs[0][...] = h2
            out_refs[1][...] = normed.astype(BF16)

    return out_proj, embed, norm, gate, finish


def _layer0_tail_kernel(sinks_ref, z_ref, kp_ref, vp_ref, vn_ref, wsp_ref, bsp_ref,
                        h_ref, p_ref, woa_ref, wob_ref, wple_ref, wg_ref, pn_ref, gn_ref, nn_ref,
                        h_out_ref, hn_out_ref, y_buf, *, tb):
    i = pl.program_id(0)

    @pl.when(i == 0)
    def _():
        y_buf[...] = jnp.zeros_like(y_buf)

    slot = lax.rem(i, 2)
    y_new, y_old = y_buf.at[slot], y_buf.at[1 - slot]
    out_proj, embed, norm, gate, finish = _post_phases(
        lambda: y_old[:, :HALF], lambda: y_old[:, HALF:], h_ref, p_ref, woa_ref, wob_ref, wple_ref, wg_ref,
        pn_ref, gn_ref, nn_ref, (h_out_ref, hn_out_ref), final=False)
    mix = [_mixer0_phases(sinks_ref, z_ref, kp_ref, vp_ref, vn_ref, wsp_ref, bsp_ref, y_new, i == 0, r)
           for r in range(tb // WINDOW)]
    assert len(mix) == 2 and POST_SPLIT == 4
    (scores0, gmlp0, pv0), (scores1, gmlp1, pv1) = mix
    for piece in (lambda: out_proj(0), lambda: out_proj(1), scores0, lambda: out_proj(2), lambda: out_proj(3),
                  gmlp0, embed, pv0, norm, lambda: gate(0), scores1, lambda: gate(1), gmlp1,
                  lambda: gate(2), pv1, lambda: gate(3), finish):
        piece()


def _layer0_tail(z, h, p, sinks, vnorm, w_sp, b_sp, w_out, w_ple, w_gate, ple_norm, gate_norm, next_norm, tb):
    s, d = h.shape
    n, nb = s // tb, tb // WINDOW
    cur = lambda i: jnp.minimum(i, n - 1)
    old = lambda i: jnp.maximum(i - 1, 0)
    prv = lambda c0: pl.BlockSpec((WINDOW, LANE), lambda i: (jnp.maximum(cur(i) * nb - 1, 0), c0 // LANE))
    full = lambda a: pl.BlockSpec(a.shape, lambda i: (0,) * a.ndim)
    const = lambda shape, r=0: pl.BlockSpec(shape, lambda i: (r, 0), pipeline_mode=pl.Buffered(1))
    per_layer = lambda rows: pl.BlockSpec((None, rows, d), lambda i: (0, 0, 0), pipeline_mode=pl.Buffered(1))
    row_block = pl.BlockSpec((tb, d), lambda i: (old(i), 0))
    vnorm = vnorm.reshape(1, HALF)
    b_t = b_sp.T
    return pl.pallas_call(
        functools.partial(_layer0_tail_kernel, tb=tb),
        grid=(n + 1,),
        in_specs=[pl.BlockSpec(memory_space=pltpu.SMEM),
                  pl.BlockSpec((tb, IN_EVEN), lambda i: (cur(i), 0)), prv(_A_K), prv(_A_V),
                  full(vnorm), full(w_sp), full(b_t),
                  row_block, pl.BlockSpec((None, tb, PLE_DIM), lambda i: (0, old(i), 0)),
                  const((HALF, d), 0), const((HALF, d), 1), per_layer(PLE_DIM), per_layer(d),
                  per_layer(1), per_layer(1), const((1, d))],
        out_specs=(row_block, row_block),
        out_shape=(jax.ShapeDtypeStruct((s, d), F32), jax.ShapeDtypeStruct((s, d), BF16)),
        scratch_shapes=[pltpu.VMEM((2, tb, d), BF16)],
        compiler_params=pltpu.CompilerParams(
            dimension_semantics=("arbitrary",), vmem_limit_bytes=VMEM_LIMIT),
        name="layer0_tail",
    )(sinks, z, z, z, vnorm, w_sp, b_t, h, p, w_out, w_out, w_ple, w_gate, ple_norm, gate_norm,
      next_norm.reshape(1, d))


def _chunk_cumsum(log_g, tb):
    t = lax.broadcasted_iota(jnp.int32, (tb, tb), 0)
    s = lax.broadcasted_iota(jnp.int32, (tb, tb), 1)
    tri = jnp.where((s <= t) & (t // LA_CHUNK == s // LA_CHUNK), 1.0, 0.0).astype(BF16)
    hi = log_g.astype(BF16)
    low = (log_g - hi.astype(F32)).astype(BF16)
    return _dot(tri, hi) + _dot(tri, low)


def _att_exact(qc, g, kbuf, gbuf, r0):
    row = lax.broadcasted_iota(jnp.int32, (LA_CHUNK, LA_CHUNK), 0)
    col = lax.broadcasted_iota(jnp.int32, (LA_CHUNK, LA_CHUNK), 1)
    sub_pos = lax.broadcasted_iota(jnp.int32, (LA_CHUNK, 1), 0) % LA_SUB
    zeros = lambda n: jnp.zeros((n, HEAD_K), F32)
    kc = kbuf[pl.ds(LA_SUB + r0, LA_CHUNK), :]
    att = jnp.zeros((LA_CHUNK, LA_CHUNK), F32)
    for d in range(LA_SUB):
        ks = kbuf[pl.ds(LA_SUB + r0 - d, LA_CHUNK), :]
        gs = gbuf[pl.ds(LA_SUB + r0 - d, LA_CHUNK), :]
        pd = jnp.where(sub_pos >= d, qc * ks * jnp.exp(g - gs), 0.0)
        att = jnp.where(col == row - d, jnp.sum(pd, axis=-1, keepdims=True), att)
    m = LA_SUB
    while m < LA_CHUNK:
        q_parts, k_parts = [], []
        for b0 in range(0, LA_CHUNK, 2 * m):
            ref_row = g[b0 + m - 1:b0 + m]
            lower, upper = slice(b0, b0 + m), slice(b0 + m, b0 + 2 * m)
            q_parts += [zeros(m), qc[upper] * jnp.exp(g[upper] - ref_row)]
            k_parts += [kc[lower] * jnp.exp(ref_row - g[lower]), zeros(m)]
        part = _dot_nt(jnp.concatenate(q_parts, axis=0).astype(BF16),
                       jnp.concatenate(k_parts, axis=0).astype(BF16))
        if 2 * m < LA_CHUNK:
            part = jnp.where(row // (2 * m) == col // (2 * m), part, 0.0)
        att = att + part
        m *= 2
    return att


def _att_factored(qc, kc, g):
    row = lax.broadcasted_iota(jnp.int32, (LA_CHUNK, LA_CHUNK), 0)
    col = lax.broadcasted_iota(jnp.int32, (LA_CHUNK, LA_CHUNK), 1)
    zeros = lambda n: jnp.zeros((n, HEAD_K), BF16)
    q_slabs, k_slabs = [], []
    for b0 in range(0, LA_CHUNK, LA_SUB):
        blk = slice(b0, b0 + LA_SUB)
        ref_row = g[b0 - 1:b0] if b0 else jnp.zeros((1, HEAD_K), F32)
        q_part = (qc[b0:] * jnp.exp(g[b0:] - ref_row)).astype(BF16)
        k_part = (kc[blk] * jnp.exp(ref_row - g[blk])).astype(BF16)
        q_slabs.append(jnp.concatenate([zeros(b0), q_part], axis=0) if b0 else q_part)
        k_slabs.append(jnp.concatenate(
            [z for z in (zeros(b0), k_part, zeros(LA_CHUNK - LA_SUB - b0)) if z.shape[0]], axis=0))
    att = _dot_nt(jnp.concatenate(q_slabs, axis=1), jnp.concatenate(k_slabs, axis=1))
    return jnp.where(col <= row, att, 0.0)


def _linear_attention_heads(qs, ks, vs, log_gs, st_ref, kbuf, gbuf, abuf, tb):
    n_heads, n_chunks = len(qs), tb // LA_CHUNK
    chunk = lambda a, c: a[c * LA_CHUNK:(c + 1) * LA_CHUNK]
    arows = lambda j, c: pl.ds((j * n_chunks + c) * LA_CHUNK, LA_CHUNK)
    gcs = [_chunk_cumsum(lg, tb) for lg in log_gs]
    steepest = functools.reduce(jnp.maximum, [-lg for lg in log_gs])
    safe = jnp.max(steepest) <= MAX_SAFE_STEP_DECAY

    @pl.when(safe)
    def _():
        for j in range(n_heads):
            for c in range(n_chunks):
                abuf[arows(j, c), :] = _att_factored(chunk(qs[j], c), chunk(ks[j], c), chunk(gcs[j], c))

    @pl.when(jnp.logical_not(safe))
    def _():
        for j in range(n_heads):
            kbuf[j, pl.ds(0, LA_SUB), :] = jnp.zeros((LA_SUB, HEAD_K), F32)
            gbuf[j, pl.ds(0, LA_SUB), :] = jnp.zeros((LA_SUB, HEAD_K), F32)
            kbuf[j, pl.ds(LA_SUB, tb), :] = ks[j]
            gbuf[j, pl.ds(LA_SUB, tb), :] = gcs[j]
            for c in range(n_chunks):
                abuf[arows(j, c), :] = _att_exact(
                    chunk(qs[j], c), chunk(gcs[j], c), kbuf.at[j], gbuf.at[j], c * LA_CHUNK)

    intra, updates, decays = {}, {}, {}
    for j in range(n_heads):
        for c in range(n_chunks):
            kc, vc, g = chunk(ks[j], c), chunk(vs[j], c), chunk(gcs[j], c)
            g_last = g[LA_CHUNK - 1:LA_CHUNK]
            intra[j, c] = _dot(abuf[arows(j, c), :].astype(BF16), vc)
            updates[j, c] = _dot_tn(vc, (kc * jnp.exp(g_last - g)).astype(BF16))
            decays[j, c] = jnp.exp(g_last)
    states = {}
    for j in range(n_heads):
        st = st_ref[j]
        for c in range(n_chunks):
            states[j, c] = st.astype(BF16)
            st = st * decays[j, c] + updates[j, c]
        st_ref[j] = st
    return [jnp.concatenate(
        [_dot_nt((chunk(qs[j], c) * jnp.exp(chunk(gcs[j], c))).astype(BF16), states[j, c]) + intra[j, c]
         for c in range(n_chunks)], axis=0) for j in range(n_heads)]


def _la_scratch(tb, dv, hp):
    return [pltpu.VMEM((hp, dv, HEAD_K), F32),
            pltpu.VMEM((hp, LA_SUB + tb, HEAD_K), F32),
            pltpu.VMEM((hp, LA_SUB + tb, HEAD_K), F32),
            pltpu.VMEM((hp * tb, LA_CHUNK), F32)]


def _head_cols(j, width):
    return slice(j * width, (j + 1) * width)


def _hgrn2_kernel(q_ref, f_ref, i_ref, gate_ref, lb_ref, on_ref, y_ref, st_ref, kbuf, gbuf, abuf, *, tb, hp):
    @pl.when(pl.program_id(1) == 0)
    def _():
        st_ref[...] = jnp.zeros_like(st_ref)

    qs, ks, vs, lgs = [], [], [], []
    for j in range(hp):
        cols = _head_cols(j, HEAD_K)
        lb = lb_ref[:, cols]
        f = lb + (1.0 - lb) * jax.nn.sigmoid(f_ref[:, cols].astype(F32))
        qs.append(q_ref[:, cols].astype(F32) * (HEAD_K ** -0.5))
        ks.append(1.0 - f)
        vs.append(i_ref[:, cols])
        lgs.append(jnp.log(f))
    outs = _linear_attention_heads(qs, ks, vs, lgs, st_ref, kbuf, gbuf, abuf, tb)
    for j in range(hp):
        cols = _head_cols(j, HEAD_V_C)
        y_ref[:, cols] = (_rms(outs[j], on_ref[:, cols]) * _silu(gate_ref[:, cols].astype(F32))).astype(BF16)


def _gla_kernel(q_ref, k_ref, v_ref, gate_ref, glr_ref, wup_ref, bg_ref, on_ref, y_ref,
                st_ref, kbuf, gbuf, abuf, *, tb, hp):
    @pl.when(pl.program_id(1) == 0)
    def _():
        st_ref[...] = jnp.zeros_like(st_ref)

    x = _dot(glr_ref[...], wup_ref[...]) + bg_ref[...]
    log_alpha = -(jnp.maximum(-x, 0.0) + jnp.log1p(jnp.exp(-jnp.abs(x)))) / GATE_LOGIT_NORM_D
    qs, ks, vs, lgs = [], [], [], []
    for j in range(hp):
        cols = _head_cols(j, HEAD_K)
        qs.append(q_ref[:, cols].astype(F32) * (HEAD_K ** -0.5))
        ks.append(k_ref[:, cols].astype(F32))
        vs.append(v_ref[:, _head_cols(j, HEAD_V_D)])
        lgs.append(log_alpha[:, cols])
    outs = _linear_attention_heads(qs, ks, vs, lgs, st_ref, kbuf, gbuf, abuf, tb)
    for j in range(hp):
        cols = _head_cols(j, HEAD_V_D)
        y_ref[:, cols] = (_rms(outs[j], on_ref[:, cols]) * _silu(gate_ref[:, cols].astype(F32))).astype(BF16)


_C_Q, _C_F, _C_I, _C_GATE = 0, 8, 16, 24
_D_Q, _D_K, _D_V, _D_GATE, _D_GLR = 32, 36, 40, 48, 56
PACKED_ODD = 57 * LANE


def _hgrn2(z, lb, onorm, tb, hp):
    s = z.shape[0]
    kw, vw = hp * HEAD_K, hp * HEAD_V_C
    col = lambda c0: pl.BlockSpec((tb, kw), lambda h, i: (i, c0 // hp + h))
    vec = lambda w: pl.BlockSpec((1, w), lambda h, i: (0, h))
    return pl.pallas_call(
        functools.partial(_hgrn2_kernel, tb=tb, hp=hp),
        grid=(N_HEADS_C // hp, s // tb),
        in_specs=[col(_C_Q), col(_C_F), col(_C_I), col(_C_GATE), vec(kw), vec(vw)],
        out_specs=pl.BlockSpec((tb, vw), lambda h, i: (i, h)),
        out_shape=jax.ShapeDtypeStruct((s, HALF), BF16),
        scratch_shapes=_la_scratch(tb, HEAD_V_C, hp),
        compiler_params=pltpu.CompilerParams(
            dimension_semantics=("arbitrary", "arbitrary"), vmem_limit_bytes=VMEM_LIMIT),
        name="hgrn2",
    )(z, z, z, z, lb.reshape(1, HALF), onorm.reshape(1, HALF))


def _gla(z, w_up, b_gate, onorm, tb, hp):
    s = z.shape[0]
    kw, vw = hp * HEAD_K, hp * HEAD_V_D
    col = lambda c0, w: pl.BlockSpec((tb, w), lambda h, i: (i, c0 * LANE // w + h))
    vec = lambda w: pl.BlockSpec((1, w), lambda h, i: (0, h))
    w_up = jnp.pad(w_up, ((0, LANE - GATE_RANK_D), (0, 0))).astype(BF16)
    return pl.pallas_call(
        functools.partial(_gla_kernel, tb=tb, hp=hp),
        grid=(N_HEADS_D // hp, s // tb),
        in_specs=[col(_D_Q, kw), col(_D_K, kw), col(_D_V, vw), col(_D_GATE, vw),
                  pl.BlockSpec((tb, LANE), lambda h, i: (i, _D_GLR)),
                  pl.BlockSpec((LANE, kw), lambda h, i: (0, h)),
                  vec(kw), vec(vw)],
        out_specs=pl.BlockSpec((tb, vw), lambda h, i: (i, h)),
        out_shape=jax.ShapeDtypeStruct((s, HALF), BF16),
        scratch_shapes=_la_scratch(tb, HEAD_V_D, hp),
        compiler_params=pltpu.CompilerParams(
            dimension_semantics=("arbitrary", "arbitrary"), vmem_limit_bytes=VMEM_LIMIT),
        name="gla",
    )(z, z, z, z, z, w_up, b_gate.reshape(1, N_HEADS_D * HEAD_K), onorm.reshape(1, HALF))


def _post_kernel(ya_ref, yb_ref, h_ref, p_ref, woa_ref, wob_ref, wple_ref, wg_ref,
                 pn_ref, gn_ref, nn_ref, *out_refs, final):
    out_proj, embed, norm, gate, finish = _post_phases(
        lambda: ya_ref[...], lambda: yb_ref[...], h_ref, p_ref, woa_ref, wob_ref, wple_ref, wg_ref,
        pn_ref, gn_ref, nn_ref, out_refs, final)
    for c in range(POST_SPLIT):
        out_proj(c)
    embed()
    norm()
    for c in range(POST_SPLIT):
        gate(c)
    finish()


def _post(ya, yb, ca, cb, h, p, layer, w_out, w_ple, w_gate, ple_norm, gate_norm, next_norm, tm, final):
    s, d = h.shape
    const = lambda shape, r=0: pl.BlockSpec(shape, lambda i: (r, 0), pipeline_mode=pl.Buffered(1))
    per_layer = lambda rows: pl.BlockSpec((None, rows, d), lambda i: (layer, 0, 0), pipeline_mode=pl.Buffered(1))
    row_block = pl.BlockSpec((tm, d), lambda i: (i, 0))
    out_f32, out_bf16 = jax.ShapeDtypeStruct((s, d), F32), jax.ShapeDtypeStruct((s, d), BF16)
    return pl.pallas_call(
        functools.partial(_post_kernel, final=final),
        grid=(s // tm,),
        in_specs=[pl.BlockSpec((tm, HALF), lambda i: (i, ca)),
                  pl.BlockSpec((tm, HALF), lambda i: (i, cb)),
                  row_block,
                  pl.BlockSpec((None, tm, PLE_DIM), lambda i: (layer, i, 0)),
                  const((HALF, d), 0), const((HALF, d), 1), per_layer(PLE_DIM), per_layer(d),
                  per_layer(1), per_layer(1), const((1, d))],
        out_specs=row_block if final else (row_block, row_block),
        out_shape=out_f32 if final else (out_f32, out_bf16),
        compiler_params=pltpu.CompilerParams(
            dimension_semantics=("arbitrary",), vmem_limit_bytes=VMEM_LIMIT),
        name="post_final" if final else "post",
    )(ya, yb, h, p, w_out, w_out, w_ple, w_gate, ple_norm, gate_norm, next_norm.reshape(1, d))


def _pack_odd_kernel(w_ref, o_ref):
    glr_start = _D_GATE * LANE
    rows = o_ref.shape[0]
    o_ref[:, :glr_start] = w_ref[:, :glr_start].astype(BF16)
    tail = w_ref[:, glr_start:]
    o_ref[:, glr_start:_D_GLR * LANE] = tail[:, GATE_RANK_D:].astype(BF16)
    o_ref[:, _D_GLR * LANE:] = jnp.concatenate(
        [tail[:, :GATE_RANK_D].astype(BF16), jnp.zeros((rows, LANE - GATE_RANK_D), BF16)], axis=1)


def _pack_odd(w, tr):
    d, n = w.shape
    return pl.pallas_call(
        _pack_odd_kernel,
        grid=(d // tr,),
        in_specs=[pl.BlockSpec((tr, n), lambda i: (i, 0))],
        out_specs=pl.BlockSpec((tr, PACKED_ODD), lambda i: (i, 0)),
        out_shape=jax.ShapeDtypeStruct((d, PACKED_ODD), BF16),
        compiler_params=pltpu.CompilerParams(
            dimension_semantics=("arbitrary",), vmem_limit_bytes=VMEM_LIMIT),
        name="pack_odd",
    )(w)


def kernel(x, p, norm_mix, w_in_even, sinks_a, vnorm_b, w_spatial_b, b_spatial_b, w_out_even,
           w_in_odd, lower_bounds_c, onorm_c, w_gate_up_d, b_gate_d, onorm_d, w_out_odd,
           w_ple_proj, ple_norm, ple_gate_norm, w_ple_gate, final_norm):
    assert x.shape[0] == 1 and x.shape[2] == D_MODEL and norm_mix.shape[0] == 2
    s = x.shape[1]
    h = x[0]
    p = p.reshape(2, s, PLE_DIM)
    w_ple, w_gate = w_ple_proj.astype(BF16), w_ple_gate.astype(BF16)
    ple_norm, gate_norm = ple_norm.reshape(2, 1, D_MODEL), ple_gate_norm.reshape(2, 1, D_MODEL)
    post = functools.partial(_post, p=p, w_ple=w_ple, w_gate=w_gate, ple_norm=ple_norm, gate_norm=gate_norm,
                             tm=256)

    z = _norm_matmul(h, norm_mix[0], w_in_even[0].astype(BF16), tm=min(s, 1024), tn=IN_EVEN // 3)
    h, hn = _layer0_tail(z, h, p, sinks_a[0], vnorm_b[0], w_spatial_b[0], b_spatial_b[0],
                         w_out_even[0].astype(BF16), w_ple, w_gate, ple_norm, gate_norm, norm_mix[1], tb=256)

    z = _norm_matmul(hn, None, _pack_odd(w_in_odd[0], tr=256), tm=min(s, 1024), tn=PACKED_ODD // 3)
    sm = jax.nn.softmax(lower_bounds_c.astype(F32), axis=0)
    lb = (jnp.cumsum(sm, axis=0) - sm[0])[1]
    yc = _hgrn2(z, lb, onorm_c[0], tb=256, hp=4)
    yd = _gla(z, w_gate_up_d[0], b_gate_d[0], onorm_d[0], tb=256, hp=2)
    out = post(yc, yd, 0, 0, h, layer=1, w_out=w_out_odd[0].astype(BF16), next_norm=final_norm, final=True)
    return out[None]
```

```python
import functools

import jax
import jax.numpy as jnp
from jax import lax
from jax.experimental import pallas as pl
from jax.experimental.pallas import tpu as pltpu

F32 = jnp.float32
BF16 = jnp.bfloat16

EPS = 1e-6
NEG_INF = -1e30

D_MODEL = 2048
PLE_DIM = 256
HALF = D_MODEL // 2
HEAD_A = 64
N_KV_A = 2
Q_PER_KV_A = 8
WINDOW = 128
N_GROUPS_B = 8
HEAD_K = 128
N_HEADS_C = 8
N_HEADS_D = 4
HEAD_V_C = 128
HEAD_V_D = 256
GATE_RANK_D = 16
GATE_LOGIT_NORM_D = 16.0
LA_CHUNK = 64
LA_SUB = 16
MAX_SAFE_STEP_DECAY = 60.0 / LA_SUB
LANE = 128
VMEM_LIMIT = 56 * 1024 * 1024


def _dot(a, b):
    return jnp.dot(a, b, preferred_element_type=F32)


def _dot_nt(a, b):
    return lax.dot_general(a, b, (((1,), (1,)), ((), ())), preferred_element_type=F32)


def _dot_tn(a, b):
    return lax.dot_general(a, b, (((0,), (0,)), ((), ())), preferred_element_type=F32)


def _rms(x, gain):
    return x * lax.rsqrt(jnp.mean(x * x, axis=-1, keepdims=True) + EPS) * gain


def _silu(x):
    return x * jax.nn.sigmoid(x)


def _gelu(x):
    return 0.5 * x * (1.0 + lax.erf(x * (2.0 ** -0.5)))


def _norm_matmul_kernel(x_ref, g_ref, w_ref, o_ref, xn_ref):
    @pl.when(pl.program_id(1) == 0)
    def _():
        xn_ref[...] = _rms(x_ref[...], g_ref[...]).astype(BF16)

    o_ref[...] = _dot(xn_ref[...], w_ref[...]).astype(o_ref.dtype)


def _matmul_kernel(x_ref, w_ref, o_ref):
    o_ref[...] = _dot(x_ref[...], w_ref[...]).astype(o_ref.dtype)


def _norm_matmul(x, gain, w, tm, tn):
    s, d = x.shape
    n = w.shape[1]
    x_spec = pl.BlockSpec((tm, d), lambda i, j: (i, 0))
    w_spec = pl.BlockSpec((d, tn), lambda i, j: (0, j))
    common = dict(
        grid=(s // tm, n // tn),
        out_specs=pl.BlockSpec((tm, tn), lambda i, j: (i, j)),
        out_shape=jax.ShapeDtypeStruct((s, n), BF16),
        compiler_params=pltpu.CompilerParams(
            dimension_semantics=("arbitrary", "arbitrary"), vmem_limit_bytes=VMEM_LIMIT))
    if gain is None:
        return pl.pallas_call(_matmul_kernel, in_specs=[x_spec, w_spec], name="matmul", **common)(x, w)
    return pl.pallas_call(
        _norm_matmul_kernel,
        in_specs=[x_spec, pl.BlockSpec((1, d), lambda i, j: (0, 0)), w_spec],
        scratch_shapes=[pltpu.VMEM((tm, d), BF16)],
        name="norm_matmul", **common)(x, gain.reshape(1, d), w)


_A_Q, _A_K, _A_V, _A_GATE, _B_U, _B_V, _B_GATE = 0, 1024, 1152, 1280, 2304, 3328, 4352
IN_EVEN = 5376


def _mixer0_phases(sinks_ref, z_ref, kp_ref, vp_ref, vn_ref, wsp_ref, bsp_ref, y_ref, first, r):
    lane = lax.broadcasted_iota(jnp.int32, (1, LANE), 1)
    lo = lane < HEAD_A
    qi = lax.broadcasted_iota(jnp.int32, (WINDOW, 2 * WINDOW), 0)
    kj = lax.broadcasted_iota(jnp.int32, (WINDOW, 2 * WINDOW), 1)
    band = (kj > qi) & (kj <= qi + WINDOW)
    tri = (lax.broadcasted_iota(jnp.int32, (WINDOW, WINDOW), 0)
           >= lax.broadcasted_iota(jnp.int32, (WINDOW, WINDOW), 1))
    zero_bf = jnp.zeros((), BF16)
    n_q = Q_PER_KV_A
    rows = pl.ds(r * WINDOW, WINDOW)
    zcols = lambda c0, w: z_ref[rows, pl.ds(c0, w)]
    st = {}

    def swap_halves(t):
        return jnp.concatenate([t[:, HEAD_A:], t[:, :HEAD_A]], axis=1)

    def both_halves(t, t_sw, h):
        return jnp.where(lo, t, t_sw) if h == 0 else jnp.where(lo, t_sw, t)

    def scores():
        if r == 0:
            k_prev, v_prev = kp_ref[...], vp_ref[...]
            st["mask"] = band & ((kj >= WINDOW) | jnp.logical_not(first))
        else:
            prev = pl.ds((r - 1) * WINDOW, WINDOW)
            k_prev, v_prev = z_ref[prev, pl.ds(_A_K, LANE)], z_ref[prev, pl.ds(_A_V, LANE)]
            st["mask"] = band
        kk = jnp.concatenate([k_prev, zcols(_A_K, LANE)], axis=0)
        vv = jnp.concatenate([v_prev, zcols(_A_V, LANE)], axis=0)
        kk_sw = swap_halves(kk)
        st["vv"], st["vv_sw"] = vv, swap_halves(vv)
        st["scores"] = []
        for h in range(N_KV_A):
            parts = []
            for j in range(n_q // 2):
                qp = zcols(_A_Q + (h * n_q + 2 * j) * HEAD_A, LANE) * jnp.asarray(HEAD_A ** -0.5, BF16)
                parts += [jnp.where(lo, qp, zero_bf), jnp.where(lo, zero_bf, qp)]
            st["scores"].append(_dot_nt(jnp.concatenate(parts, axis=0), both_halves(kk, kk_sw, h)))

    def gmlp():
        u = _gelu(zcols(_B_U, HALF).astype(F32))
        vg = _rms(_gelu(zcols(_B_V, HALF).astype(F32)), vn_ref[...]).astype(BF16)
        gb = zcols(_B_GATE, HALF).astype(F32)
        for g in range(N_GROUPS_B):
            cols = slice(g * LANE, (g + 1) * LANE)
            w = jnp.where(tri, wsp_ref[g], 0.0).astype(BF16)
            mixed = _dot(w, vg[:, cols]) + bsp_ref[:, g:g + 1]
            y_ref[rows, pl.ds(HALF + g * LANE, LANE)] = (u[:, cols] * mixed * _silu(gb[:, cols])).astype(BF16)

    def softmax_pv():
        mask, vv, vv_sw = st["mask"], st["vv"], st["vv_sw"]
        for h in range(N_KV_A):
            ps, rs = [], []
            for i in range(n_q):
                sink = sinks_ref[h * n_q + i]
                sc = jnp.where(mask, st["scores"][h][i * WINDOW:(i + 1) * WINDOW], NEG_INF)
                m = jnp.maximum(jnp.max(sc, axis=-1, keepdims=True), sink)
                p = jnp.exp(sc - m)
                rs.append(1.0 / (jnp.sum(p, axis=-1, keepdims=True) + jnp.exp(sink - m)))
                ps.append(p.astype(BF16))
            v_both = both_halves(vv, vv_sw, h)
            v2 = jnp.concatenate([jnp.where(lo, v_both, zero_bf), jnp.where(lo, zero_bf, v_both)], axis=0)
            p_pairs = jnp.concatenate(
                [jnp.concatenate(ps[2 * j:2 * j + 2], axis=1) for j in range(n_q // 2)], axis=0)
            o = _dot(p_pairs, v2)
            for j in range(n_q // 2):
                c0 = (h * n_q + 2 * j) * HEAD_A
                oj = o[j * WINDOW:(j + 1) * WINDOW] * jnp.where(lo, rs[2 * j], rs[2 * j + 1])
                g = zcols(_A_GATE + c0, LANE).astype(F32)
                y_ref[rows, pl.ds(c0, LANE)] = (oj * _silu(g)).astype(BF16)

    return scores, gmlp, softmax_pv


POST_SPLIT = 4


def _post_phases(ya, yb, h_ref, p_ref, woa_ref, wob_ref, wple_ref, wg_ref, pn_ref, gn_ref, nn_ref, out_refs, final):
    cw = D_MODEL // POST_SPLIT
    st = {"h1": [], "gate": []}

    def out_proj(c):
        cols = pl.ds(c * cw, cw)
        st["h1"].append(h_ref[:, cols] + _dot(ya(), woa_ref[:, cols]) + _dot(yb(), wob_ref[:, cols]))

    def embed():
        st["e"] = _rms(_dot(p_ref[...].astype(BF16), wple_ref[...]), pn_ref[...])

    def norm():
        st["h1"] = jnp.concatenate(st["h1"], axis=1)
        st["hn1"] = _rms(st["h1"], gn_ref[...]).astype(BF16)

    def gate(c):
        st["gate"].append(jax.nn.sigmoid(_dot(st["hn1"], wg_ref[:, pl.ds(c * cw, cw)])))

    def finish():
        h2 = st["h1"] + st["e"] * jnp.concatenate(st["gate"], axis=1)
        normed = _rms(h2, nn_ref[...])
        if final:
            out_refs[0][...] = normed
        else:
            out_refs[0][...] = h2
            out_refs[1][...] = normed.astype(BF16)

    return out_proj, embed, norm, gate, finish


def _layer0_tail_kernel(sinks_ref, z_ref, kp_ref, vp_ref, vn_ref, wsp_ref, bsp_ref,
                        h_ref, p_ref, woa_ref, wob_ref, wple_ref, wg_ref, pn_ref, gn_ref, nn_ref,
                        h_out_ref, hn_out_ref, y_buf, *, tb):
    i = pl.program_id(0)

    @pl.when(i == 0)
    def _():
        y_buf[...] = jnp.zeros_like(y_buf)

    slot = lax.rem(i, 2)
    y_new, y_old = y_buf.at[slot], y_buf.at[1 - slot]
    out_proj, embed, norm, gate, finish = _post_phases(
        lambda: y_old[:, :HALF], lambda: y_old[:, HALF:], h_ref, p_ref, woa_ref, wob_ref, wple_ref, wg_ref,
        pn_ref, gn_ref, nn_ref, (h_out_ref, hn_out_ref), final=False)
    mix = [_mixer0_phases(sinks_ref, z_ref, kp_ref, vp_ref, vn_ref, wsp_ref, bsp_ref, y_new, i == 0, r)
           for r in range(tb // WINDOW)]
    assert len(mix) == 2 and POST_SPLIT == 4
    (scores0, gmlp0, pv0), (scores1, gmlp1, pv1) = mix
    for piece in (lambda: out_proj(0), lambda: out_proj(1), scores0, lambda: out_proj(2), lambda: out_proj(3),
                  gmlp0, embed, pv0, norm, lambda: gate(0), scores1, lambda: gate(1), gmlp1,
                  lambda: gate(2), pv1, lambda: gate(3), finish):
        piece()


def _layer0_tail(z, h, p, sinks, vnorm, w_sp, b_sp, w_out, w_ple, w_gate, ple_norm, gate_norm, next_norm, tb):
    s, d = h.shape
    n, nb = s // tb, tb // WINDOW
    cur = lambda i: jnp.minimum(i, n - 1)
    old = lambda i: jnp.maximum(i - 1, 0)
    prv = lambda c0: pl.BlockSpec((WINDOW, LANE), lambda i: (jnp.maximum(cur(i) * nb - 1, 0), c0 // LANE))
    full = lambda a: pl.BlockSpec(a.shape, lambda i: (0,) * a.ndim)
    const = lambda shape, r=0: pl.BlockSpec(shape, lambda i: (r, 0), pipeline_mode=pl.Buffered(1))
    per_layer = lambda rows: pl.BlockSpec((None, rows, d), lambda i: (0, 0, 0), pipeline_mode=pl.Buffered(1))
    row_block = pl.BlockSpec((tb, d), lambda i: (old(i), 0))
    vnorm = vnorm.reshape(1, HALF)
    b_t = b_sp.T
    return pl.pallas_call(
        functools.partial(_layer0_tail_kernel, tb=tb),
        grid=(n + 1,),
        in_specs=[pl.BlockSpec(memory_space=pltpu.SMEM),
                  pl.BlockSpec((tb, IN_EVEN), lambda i: (cur(i), 0)), prv(_A_K), prv(_A_V),
                  full(vnorm), full(w_sp), full(b_t),
                  row_block, pl.BlockSpec((None, tb, PLE_DIM), lambda i: (0, old(i), 0)),
                  const((HALF, d), 0), const((HALF, d), 1), per_layer(PLE_DIM), per_layer(d),
                  per_layer(1), per_layer(1), const((1, d))],
        out_specs=(row_block, row_block),
        out_shape=(jax.ShapeDtypeStruct((s, d), F32), jax.ShapeDtypeStruct((s, d), BF16)),
        scratch_shapes=[pltpu.VMEM((2, tb, d), BF16)],
        compiler_params=pltpu.CompilerParams(
            dimension_semantics=("arbitrary",), vmem_limit_bytes=VMEM_LIMIT),
        name="layer0_tail",
    )(sinks, z, z, z, vnorm, w_sp, b_t, h, p, w_out, w_out, w_ple, w_gate, ple_norm, gate_norm,
      next_norm.reshape(1, d))


def _chunk_cumsum(log_g, tb):
    t = lax.broadcasted_iota(jnp.int32, (tb, tb), 0)
    s = lax.broadcasted_iota(jnp.int32, (tb, tb), 1)
    tri = jnp.where((s <= t) & (t // LA_CHUNK == s // LA_CHUNK), 1.0, 0.0).astype(BF16)
    hi = log_g.astype(BF16)
    low = (log_g - hi.astype(F32)).astype(BF16)
    return _dot(tri, hi) + _dot(tri, low)


def _att_exact(qc, g, kbuf, gbuf, r0):
    row = lax.broadcasted_iota(jnp.int32, (LA_CHUNK, LA_CHUNK), 0)
    col = lax.broadcasted_iota(jnp.int32, (LA_CHUNK, LA_CHUNK), 1)
    sub_pos = lax.broadcasted_iota(jnp.int32, (LA_CHUNK, 1), 0) % LA_SUB
    zeros = lambda n: jnp.zeros((n, HEAD_K), F32)
    kc = kbuf[pl.ds(LA_SUB + r0, LA_CHUNK), :]
    att = jnp.zeros((LA_CHUNK, LA_CHUNK), F32)
    for d in range(LA_SUB):
        ks = kbuf[pl.ds(LA_SUB + r0 - d, LA_CHUNK), :]
        gs = gbuf[pl.ds(LA_SUB + r0 - d, LA_CHUNK), :]
        pd = jnp.where(sub_pos >= d, qc * ks * jnp.exp(g - gs), 0.0)
        att = jnp.where(col == row - d, jnp.sum(pd, axis=-1, keepdims=True), att)
    m = LA_SUB
    while m < LA_CHUNK:
        q_parts, k_parts = [], []
        for b0 in range(0, LA_CHUNK, 2 * m):
            ref_row = g[b0 + m - 1:b0 + m]
            lower, upper = slice(b0, b0 + m), slice(b0 + m, b0 + 2 * m)
            q_parts += [zeros(m), qc[upper] * jnp.exp(g[upper] - ref_row)]
            k_parts += [kc[lower] * jnp.exp(ref_row - g[lower]), zeros(m)]
        part = _dot_nt(jnp.concatenate(q_parts, axis=0).astype(BF16),
                       jnp.concatenate(k_parts, axis=0).astype(BF16))
        if 2 * m < LA_CHUNK:
            part = jnp.where(row // (2 * m) == col // (2 * m), part, 0.0)
        att = att + part
        m *= 2
    return att


def _att_factored(qc, kc, g):
    row = lax.broadcasted_iota(jnp.int32, (LA_CHUNK, LA_CHUNK), 0)
    col = lax.broadcasted_iota(jnp.int32, (LA_CHUNK, LA_CHUNK), 1)
    zeros = lambda n: jnp.zeros((n, HEAD_K), BF16)
    q_slabs, k_slabs = [], []
    for b0 in range(0, LA_CHUNK, LA_SUB):
        blk = slice(b0, b0 + LA_SUB)
        ref_row = g[b0 - 1:b0] if b0 else jnp.zeros((1, HEAD_K), F32)
        q_part = (qc[b0:] * jnp.exp(g[b0:] - ref_row)).astype(BF16)
        k_part = (kc[blk] * jnp.exp(ref_row - g[blk])).astype(BF16)
        q_slabs.append(jnp.concatenate([zeros(b0), q_part], axis=0) if b0 else q_part)
        k_slabs.append(jnp.concatenate(
            [z for z in (zeros(b0), k_part, zeros(LA_CHUNK - LA_SUB - b0)) if z.shape[0]], axis=0))
    att = _dot_nt(jnp.concatenate(q_slabs, axis=1), jnp.concatenate(k_slabs, axis=1))
    return jnp.where(col <= row, att, 0.0)


def _linear_attention_heads(qs, ks, vs, log_gs, st_ref, kbuf, gbuf, abuf, tb):
    n_heads, n_chunks = len(qs), tb // LA_CHUNK
    chunk = lambda a, c: a[c * LA_CHUNK:(c + 1) * LA_CHUNK]
    arows = lambda j, c: pl.ds((j * n_chunks + c) * LA_CHUNK, LA_CHUNK)
    gcs = [_chunk_cumsum(lg, tb) for lg in log_gs]
    steepest = functools.reduce(jnp.maximum, [-lg for lg in log_gs])
    safe = jnp.max(steepest) <= MAX_SAFE_STEP_DECAY

    @pl.when(safe)
    def _():
        for j in range(n_heads):
            for c in range(n_chunks):
                abuf[arows(j, c), :] = _att_factored(chunk(qs[j], c), chunk(ks[j], c), chunk(gcs[j], c))

    @pl.when(jnp.logical_not(safe))
    def _():
        for j in range(n_heads):
            kbuf[j, pl.ds(0, LA_SUB), :] = jnp.zeros((LA_SUB, HEAD_K), F32)
            gbuf[j, pl.ds(0, LA_SUB), :] = jnp.zeros((LA_SUB, HEAD_K), F32)
            kbuf[j, pl.ds(LA_SUB, tb), :] = ks[j]
            gbuf[j, pl.ds(LA_SUB, tb), :] = gcs[j]
            for c in range(n_chunks):
                abuf[arows(j, c), :] = _att_exact(
                    chunk(qs[j], c), chunk(gcs[j], c), kbuf.at[j], gbuf.at[j], c * LA_CHUNK)

    intra, updates, decays = {}, {}, {}
    for j in range(n_heads):
        for c in range(n_chunks):
            kc, vc, g = chunk(ks[j], c), chunk(vs[j], c), chunk(gcs[j], c)
            g_last = g[LA_CHUNK - 1:LA_CHUNK]
            intra[j, c] = _dot(abuf[arows(j, c), :].astype(BF16), vc)
            updates[j, c] = _dot_tn(vc, (kc * jnp.exp(g_last - g)).astype(BF16))
            decays[j, c] = jnp.exp(g_last)
    states = {}
    for j in range(n_heads):
        st = st_ref[j]
        for c in range(n_chunks):
            states[j, c] = st.astype(BF16)
            st = st * decays[j, c] + updates[j, c]
        st_ref[j] = st
    return [jnp.concatenate(
        [_dot_nt((chunk(qs[j], c) * jnp.exp(chunk(gcs[j], c))).astype(BF16), states[j, c]) + intra[j, c]
         for c in range(n_chunks)], axis=0) for j in range(n_heads)]


def _la_scratch(tb, dv, hp):
    return [pltpu.VMEM((hp, dv, HEAD_K), F32),
            pltpu.VMEM((hp, LA_SUB + tb, HEAD_K), F32),
            pltpu.VMEM((hp, LA_SUB + tb, HEAD_K), F32),
            pltpu.VMEM((hp * tb, LA_CHUNK), F32)]


def _head_cols(j, width):
    return slice(j * width, (j + 1) * width)


def _hgrn2_kernel(q_ref, f_ref, i_ref, gate_ref, lb_ref, on_ref, y_ref, st_ref, kbuf, gbuf, abuf, *, tb, hp):
    @pl.when(pl.program_id(1) == 0)
    def _():
        st_ref[...] = jnp.zeros_like(st_ref)

    qs, ks, vs, lgs = [], [], [], []
    for j in range(hp):
        cols = _head_cols(j, HEAD_K)
        lb = lb_ref[:, cols]
        f = lb + (1.0 - lb) * jax.nn.sigmoid(f_ref[:, cols].astype(F32))
        qs.append(q_ref[:, cols].astype(F32) * (HEAD_K ** -0.5))
        ks.append(1.0 - f)
        vs.append(i_ref[:, cols])
        lgs.append(jnp.log(f))
    outs = _linear_attention_heads(qs, ks, vs, lgs, st_ref, kbuf, gbuf, abuf, tb)
    for j in range(hp):
        cols = _head_cols(j, HEAD_V_C)
        y_ref[:, cols] = (_rms(outs[j], on_ref[:, cols]) * _silu(gate_ref[:, cols].astype(F32))).astype(BF16)


def _gla_kernel(q_ref, k_ref, v_ref, gate_ref, glr_ref, wup_ref, bg_ref, on_ref, y_ref,
                st_ref, kbuf, gbuf, abuf, *, tb, hp):
    @pl.when(pl.program_id(1) == 0)
    def _():
        st_ref[...] = jnp.zeros_like(st_ref)

    x = _dot(glr_ref[...], wup_ref[...]) + bg_ref[...]
    log_alpha = -(jnp.maximum(-x, 0.0) + jnp.log1p(jnp.exp(-jnp.abs(x)))) / GATE_LOGIT_NORM_D
    qs, ks, vs, lgs = [], [], [], []
    for j in range(hp):
        cols = _head_cols(j, HEAD_K)
        qs.append(q_ref[:, cols].astype(F32) * (HEAD_K ** -0.5))
        ks.append(k_ref[:, cols].astype(F32))
        vs.append(v_ref[:, _head_cols(j, HEAD_V_D)])
        lgs.append(log_alpha[:, cols])
    outs = _linear_attention_heads(qs, ks, vs, lgs, st_ref, kbuf, gbuf, abuf, tb)
    for j in range(hp):
        cols = _head_cols(j, HEAD_V_D)
        y_ref[:, cols] = (_rms(outs[j], on_ref[:, cols]) * _silu(gate_ref[:, cols].astype(F32))).astype(BF16)


_C_Q, _C_F, _C_I, _C_GATE = 0, 8, 16, 24
_D_Q, _D_K, _D_V, _D_GATE, _D_GLR = 32, 36, 40, 48, 56
PACKED_ODD = 57 * LANE


def _hgrn2(z, lb, onorm, tb, hp):
    s = z.shape[0]
    kw, vw = hp * HEAD_K, hp * HEAD_V_C
    col = lambda c0: pl.BlockSpec((tb, kw), lambda h, i: (i, c0 // hp + h))
    vec = lambda w: pl.BlockSpec((1, w), lambda h, i: (0, h))
    return pl.pallas_call(
        functools.partial(_hgrn2_kernel, tb=tb, hp=hp),
        grid=(N_HEADS_C // hp, s // tb),
        in_specs=[col(_C_Q), col(_C_F), col(_C_I), col(_C_GATE), vec(kw), vec(vw)],
        out_specs=pl.BlockSpec((tb, vw), lambda h, i: (i, h)),
        out_shape=jax.ShapeDtypeStruct((s, HALF), BF16),
        scratch_shapes=_la_scratch(tb, HEAD_V_C, hp),
        compiler_params=pltpu.CompilerParams(
            dimension_semantics=("arbitrary", "arbitrary"), vmem_limit_bytes=VMEM_LIMIT),
        name="hgrn2",
    )(z, z, z, z, lb.reshape(1, HALF), onorm.reshape(1, HALF))


def _gla(z, w_up, b_gate, onorm, tb, hp):
    s = z.shape[0]
    kw, vw = hp * HEAD_K, hp * HEAD_V_D
    col = lambda c0, w: pl.BlockSpec((tb, w), lambda h, i: (i, c0 * LANE // w + h))
    vec = lambda w: pl.BlockSpec((1, w), lambda h, i: (0, h))
    w_up = jnp.pad(w_up, ((0, LANE - GATE_RANK_D), (0, 0))).astype(BF16)
    return pl.pallas_call(
        functools.partial(_gla_kernel, tb=tb, hp=hp),
        grid=(N_HEADS_D // hp, s // tb),
        in_specs=[col(_D_Q, kw), col(_D_K, kw), col(_D_V, vw), col(_D_GATE, vw),
                  pl.BlockSpec((tb, LANE), lambda h, i: (i, _D_GLR)),
                  pl.BlockSpec((LANE, kw), lambda h, i: (0, h)),
                  vec(kw), vec(vw)],
        out_specs=pl.BlockSpec((tb, vw), lambda h, i: (i, h)),
        out_shape=jax.ShapeDtypeStruct((s, HALF), BF16),
        scratch_shapes=_la_scratch(tb, HEAD_V_D, hp),
        compiler_params=pltpu.CompilerParams(
            dimension_semantics=("arbitrary", "arbitrary"), vmem_limit_bytes=VMEM_LIMIT),
        name="gla",
    )(z, z, z, z, z, w_up, b_gate.reshape(1, N_HEADS_D * HEAD_K), onorm.reshape(1, HALF))


def _post_kernel(ya_ref, yb_ref, h_ref, p_ref, woa_ref, wob_ref, wple_ref, wg_ref,
                 pn_ref, gn_ref, nn_ref, *out_refs, final):
    out_proj, embed, norm, gate, finish = _post_phases(
        lambda: ya_ref[...], lambda: yb_ref[...], h_ref, p_ref, woa_ref, wob_ref, wple_ref, wg_ref,
        pn_ref, gn_ref, nn_ref, out_refs, final)
    for c in range(POST_SPLIT):
        out_proj(c)
    embed()
    norm()
    for c in range(POST_SPLIT):
        gate(c)
    finish()


def _post(ya, yb, ca, cb, h, p, layer, w_out, w_ple, w_gate, ple_norm, gate_norm, next_norm, tm, final):
    s, d = h.shape
    const = lambda shape, r=0: pl.BlockSpec(shape, lambda i: (r, 0), pipeline_mode=pl.Buffered(1))
    per_layer = lambda rows: pl.BlockSpec((None, rows, d), lambda i: (layer, 0, 0), pipeline_mode=pl.Buffered(1))
    row_block = pl.BlockSpec((tm, d), lambda i: (i, 0))
    out_f32, out_bf16 = jax.ShapeDtypeStruct((s, d), F32), jax.ShapeDtypeStruct((s, d), BF16)
    return pl.pallas_call(
        functools.partial(_post_kernel, final=final),
        grid=(s // tm,),
        in_specs=[pl.BlockSpec((tm, HALF), lambda i: (i, ca)),
                  pl.BlockSpec((tm, HALF), lambda i: (i, cb)),
                  row_block,
                  pl.BlockSpec((None, tm, PLE_DIM), lambda i: (layer, i, 0)),
                  const((HALF, d), 0), const((HALF, d), 1), per_layer(PLE_DIM), per_layer(d),
                  per_layer(1), per_layer(1), const((1, d))],
        out_specs=row_block if final else (row_block, row_block),
        out_shape=out_f32 if final else (out_f32, out_bf16),
        compiler_params=pltpu.CompilerParams(
            dimension_semantics=("arbitrary",), vmem_limit_bytes=VMEM_LIMIT),
        name="post_final" if final else "post",
    )(ya, yb, h, p, w_out, w_out, w_ple, w_gate, ple_norm, gate_norm, next_norm.reshape(1, d))


def _pack_odd_kernel(a_ref, b_ref, o_ref):
    j = pl.program_id(0)

    def emit(t):
        o_ref[...] = t.T.astype(BF16)

    @pl.when(j < _D_GATE)
    def _():
        emit(a_ref[...])

    @pl.when((j >= _D_GATE) & (j < _D_GLR))
    def _():
        emit(jnp.concatenate([a_ref[GATE_RANK_D:, :], b_ref[:GATE_RANK_D, :]], axis=0))

    @pl.when(j == _D_GLR)
    def _():
        emit(jnp.concatenate(
            [a_ref[:GATE_RANK_D, :], jnp.zeros((LANE - GATE_RANK_D, a_ref.shape[1]), F32)], axis=0))


def _pack_odd(w_t):
    n, d = w_t.shape
    n_blocks = PACKED_ODD // LANE
    src_a = lambda j: jnp.where(j == _D_GLR, _D_GATE, j)
    src_b = lambda j: jnp.clip(j + 1, _D_GATE + 1, pl.cdiv(n, LANE) - 1)
    return pl.pallas_call(
        _pack_odd_kernel,
        grid=(n_blocks,),
        in_specs=[pl.BlockSpec((LANE, d), lambda j: (src_a(j), 0)),
                  pl.BlockSpec((LANE, d), lambda j: (src_b(j), 0))],
        out_specs=pl.BlockSpec((d, LANE), lambda j: (0, j)),
        out_shape=jax.ShapeDtypeStruct((d, PACKED_ODD), BF16),
        compiler_params=pltpu.CompilerParams(
            dimension_semantics=("arbitrary",), vmem_limit_bytes=VMEM_LIMIT),
        name="pack_odd",
    )(w_t, w_t)


def kernel(x, p, norm_mix, w_in_even, sinks_a, vnorm_b, w_spatial_b, b_spatial_b, w_out_even,
           w_in_odd, lower_bounds_c, onorm_c, w_gate_up_d, b_gate_d, onorm_d, w_out_odd,
           w_ple_proj, ple_norm, ple_gate_norm, w_ple_gate, final_norm):
    assert x.shape[0] == 1 and x.shape[2] == D_MODEL and norm_mix.shape[0] == 2
    s = x.shape[1]
    h = x[0]
    p = p.reshape(2, s, PLE_DIM)
    w_ple, w_gate = w_ple_proj.astype(BF16), w_ple_gate.astype(BF16)
    ple_norm, gate_norm = ple_norm.reshape(2, 1, D_MODEL), ple_gate_norm.reshape(2, 1, D_MODEL)
    post = functools.partial(_post, p=p, w_ple=w_ple, w_gate=w_gate, ple_norm=ple_norm, gate_norm=gate_norm,
                             tm=256)

    z = _norm_matmul(h, norm_mix[0], w_in_even[0].astype(BF16), tm=min(s, 1024), tn=IN_EVEN // 3)
    h, hn = _layer0_tail(z, h, p, sinks_a[0], vnorm_b[0], w_spatial_b[0], b_spatial_b[0],
                         w_out_even[0].astype(BF16), w_ple, w_gate, ple_norm, gate_norm, norm_mix[1], tb=256)

    z = _norm_matmul(hn, None, _pack_odd(w_in_odd[0].T), tm=min(s, 1024), tn=PACKED_ODD // 3)
    sm = jax.nn.softmax(lower_bounds_c.astype(F32), axis=0)
    lb = (jnp.cumsum(sm, axis=0) - sm[0])[1]
    yc = _hgrn2(z, lb, onorm_c[0], tb=256, hp=4)
    yd = _gla(z, w_gate_up_d[0], b_gate_d[0], onorm_d[0], tb=256, hp=2)
    out = post(yc, yd, 0, 0, h, layer=1, w_out=w_out_odd[0].astype(BF16), next_norm=final_norm, final=True)
    return out[None]
```

```python
import functools
import math

import jax
import jax.numpy as jnp
from jax import lax
from jax.experimental import pallas as pl
from jax.experimental.pallas import tpu as pltpu

F32 = jnp.float32
BF16 = jnp.bfloat16

EPS = 1e-6
NEG_INF = -1e30

D_MODEL = 2048
PLE_DIM = 256
HALF = D_MODEL // 2
HEAD_A = 64
N_KV_A = 2
Q_PER_KV_A = 8
WINDOW = 128
N_GROUPS_B = 8
HEAD_K = 128
N_HEADS_C = 8
N_HEADS_D = 4
HEAD_V_C = 128
HEAD_V_D = 256
GATE_RANK_D = 16
GATE_LOGIT_NORM_D = 16.0
LA_CHUNK = 64
LA_SUB = 16
MAX_SAFE_STEP_DECAY = 60.0 / LA_SUB
LANE = 128
VMEM_LIMIT = 56 * 1024 * 1024


def _dot(a, b):
    return jnp.dot(a, b, preferred_element_type=F32)


def _dot_nt(a, b):
    return lax.dot_general(a, b, (((1,), (1,)), ((), ())), preferred_element_type=F32)


def _dot_tn(a, b):
    return lax.dot_general(a, b, (((0,), (0,)), ((), ())), preferred_element_type=F32)


def _rms(x, gain):
    return x * lax.rsqrt(jnp.mean(x * x, axis=-1, keepdims=True) + EPS) * gain


def _silu(x):
    return x * jax.nn.sigmoid(x)


def _gelu(x):
    return 0.5 * x * (1.0 + lax.erf(x * (2.0 ** -0.5)))


def _norm_matmul_kernel(x_ref, g_ref, w_ref, o_ref, xn_ref):
    @pl.when(pl.program_id(1) == 0)
    def _():
        xn_ref[...] = _rms(x_ref[...], g_ref[...]).astype(BF16)

    o_ref[...] = _dot(xn_ref[...], w_ref[...]).astype(o_ref.dtype)


def _matmul_kernel(x_ref, w_ref, o_ref):
    o_ref[...] = _dot(x_ref[...], w_ref[...]).astype(o_ref.dtype)


def _norm_matmul(x, gain, w, tm, tn):
    s, d = x.shape
    n = w.shape[1]
    x_spec = pl.BlockSpec((tm, d), lambda i, j: (i, 0))
    w_spec = pl.BlockSpec((d, tn), lambda i, j: (0, j))
    common = dict(
        grid=(s // tm, n // tn),
        out_specs=pl.BlockSpec((tm, tn), lambda i, j: (i, j)),
        out_shape=jax.ShapeDtypeStruct((s, n), BF16),
        compiler_params=pltpu.CompilerParams(
            dimension_semantics=("arbitrary", "arbitrary"), vmem_limit_bytes=VMEM_LIMIT))
    if gain is None:
        return pl.pallas_call(_matmul_kernel, in_specs=[x_spec, w_spec], name="matmul", **common)(x, w)
    return pl.pallas_call(
        _norm_matmul_kernel,
        in_specs=[x_spec, pl.BlockSpec((1, d), lambda i, j: (0, 0)), w_spec],
        scratch_shapes=[pltpu.VMEM((tm, d), BF16)],
        name="norm_matmul", **common)(x, gain.reshape(1, d), w)


_A_Q, _A_K, _A_V, _A_GATE, _B_U, _B_V, _B_GATE = 0, 1024, 1152, 1280, 2304, 3328, 4352
IN_EVEN = 5376


def _mixer0_phases(sinks_ref, z_ref, kp_ref, vp_ref, vn_ref, wsp_ref, bsp_ref, y_ref, first, r):
    lane = lax.broadcasted_iota(jnp.int32, (1, LANE), 1)
    lo = lane < HEAD_A
    qi = lax.broadcasted_iota(jnp.int32, (WINDOW, 2 * WINDOW), 0)
    kj = lax.broadcasted_iota(jnp.int32, (WINDOW, 2 * WINDOW), 1)
    band = (kj > qi) & (kj <= qi + WINDOW)
    tri = (lax.broadcasted_iota(jnp.int32, (WINDOW, WINDOW), 0)
           >= lax.broadcasted_iota(jnp.int32, (WINDOW, WINDOW), 1))
    zero_bf = jnp.zeros((), BF16)
    n_q = Q_PER_KV_A
    rows = pl.ds(r * WINDOW, WINDOW)
    zcols = lambda c0, w: z_ref[rows, pl.ds(c0, w)]
    st = {}

    def swap_halves(t):
        return jnp.concatenate([t[:, HEAD_A:], t[:, :HEAD_A]], axis=1)

    def both_halves(t, t_sw, h):
        return jnp.where(lo, t, t_sw) if h == 0 else jnp.where(lo, t_sw, t)

    def scores():
        if r == 0:
            k_prev, v_prev = kp_ref[...], vp_ref[...]
            st["mask"] = band & ((kj >= WINDOW) | jnp.logical_not(first))
        else:
            prev = pl.ds((r - 1) * WINDOW, WINDOW)
            k_prev, v_prev = z_ref[prev, pl.ds(_A_K, LANE)], z_ref[prev, pl.ds(_A_V, LANE)]
            st["mask"] = band
        kk = jnp.concatenate([k_prev, zcols(_A_K, LANE)], axis=0)
        vv = jnp.concatenate([v_prev, zcols(_A_V, LANE)], axis=0)
        kk_sw = swap_halves(kk)
        st["vv"], st["vv_sw"] = vv, swap_halves(vv)
        st["scores"] = []
        for h in range(N_KV_A):
            parts = []
            for j in range(n_q // 2):
                qp = zcols(_A_Q + (h * n_q + 2 * j) * HEAD_A, LANE) * jnp.asarray(HEAD_A ** -0.5, BF16)
                parts += [jnp.where(lo, qp, zero_bf), jnp.where(lo, zero_bf, qp)]
            st["scores"].append(_dot_nt(jnp.concatenate(parts, axis=0), both_halves(kk, kk_sw, h)))

    def gmlp():
        u = _gelu(zcols(_B_U, HALF).astype(F32))
        vg = _rms(_gelu(zcols(_B_V, HALF).astype(F32)), vn_ref[...]).astype(BF16)
        gb = zcols(_B_GATE, HALF).astype(F32)
        for g in range(N_GROUPS_B):
            cols = slice(g * LANE, (g + 1) * LANE)
            w = jnp.where(tri, wsp_ref[g], 0.0).astype(BF16)
            mixed = _dot(w, vg[:, cols]) + bsp_ref[:, g:g + 1]
            y_ref[rows, pl.ds(HALF + g * LANE, LANE)] = (u[:, cols] * mixed * _silu(gb[:, cols])).astype(BF16)

    def softmax_pv():
        mask, vv, vv_sw = st["mask"], st["vv"], st["vv_sw"]
        for h in range(N_KV_A):
            ps, rs = [], []
            for i in range(n_q):
                sink = sinks_ref[h * n_q + i]
                sc = jnp.where(mask, st["scores"][h][i * WINDOW:(i + 1) * WINDOW], NEG_INF)
                m = jnp.maximum(jnp.max(sc, axis=-1, keepdims=True), sink)
                p = jnp.exp(sc - m)
                rs.append(1.0 / (jnp.sum(p, axis=-1, keepdims=True) + jnp.exp(sink - m)))
                ps.append(p.astype(BF16))
            v_both = both_halves(vv, vv_sw, h)
            v2 = jnp.concatenate([jnp.where(lo, v_both, zero_bf), jnp.where(lo, zero_bf, v_both)], axis=0)
            p_pairs = jnp.concatenate(
                [jnp.concatenate(ps[2 * j:2 * j + 2], axis=1) for j in range(n_q // 2)], axis=0)
            o = _dot(p_pairs, v2)
            for j in range(n_q // 2):
                c0 = (h * n_q + 2 * j) * HEAD_A
                oj = o[j * WINDOW:(j + 1) * WINDOW] * jnp.where(lo, rs[2 * j], rs[2 * j + 1])
                g = zcols(_A_GATE + c0, LANE).astype(F32)
                y_ref[rows, pl.ds(c0, LANE)] = (oj * _silu(g)).astype(BF16)

    return scores, gmlp, softmax_pv


POST_SPLIT = 4


def _post_phases(ya, yb, h_ref, p_ref, woa_ref, wob_ref, wple_ref, wg_ref, pn_ref, gn_ref, nn_ref, out_refs, final):
    cw = D_MODEL // POST_SPLIT
    st = {"h1": [], "gate": []}

    def out_proj(c):
        cols = pl.ds(c * cw, cw)
        st["h1"].append(h_ref[:, cols] + _dot(ya(), woa_ref[:, cols]) + _dot(yb(), wob_ref[:, cols]))

    def embed():
        st["e"] = _rms(_dot(p_ref[...].astype(BF16), wple_ref[...]), pn_ref[...])

    def norm():
        st["h1"] = jnp.concatenate(st["h1"], axis=1)
        st["hn1"] = _rms(st["h1"], gn_ref[...]).astype(BF16)

    def gate(c):
        st["gate"].append(jax.nn.sigmoid(_dot(st["hn1"], wg_ref[:, pl.ds(c * cw, cw)])))

    def finish():
        h2 = st["h1"] + st["e"] * jnp.concatenate(st["gate"], axis=1)
        normed = _rms(h2, nn_ref[...])
        if final:
            out_refs[0][...] = normed
        else:
            out_refs[0][...] = h2
            out_refs[1][...] = normed.astype(BF16)

    return out_proj, embed, norm, gate, finish


def _layer0_tail_kernel(sinks_ref, z_ref, kp_ref, vp_ref, vn_ref, wsp_ref, bsp_ref,
                        h_ref, p_ref, woa_ref, wob_ref, wple_ref, wg_ref, pn_ref, gn_ref, nn_ref,
                        h_out_ref, hn_out_ref, y_buf, *, tb):
    i = pl.program_id(0)

    @pl.when(i == 0)
    def _():
        y_buf[...] = jnp.zeros_like(y_buf)

    slot = lax.rem(i, 2)
    y_new, y_old = y_buf.at[slot], y_buf.at[1 - slot]
    out_proj, embed, norm, gate, finish = _post_phases(
        lambda: y_old[:, :HALF], lambda: y_old[:, HALF:], h_ref, p_ref, woa_ref, wob_ref, wple_ref, wg_ref,
        pn_ref, gn_ref, nn_ref, (h_out_ref, hn_out_ref), final=False)
    mix = [_mixer0_phases(sinks_ref, z_ref, kp_ref, vp_ref, vn_ref, wsp_ref, bsp_ref, y_new, i == 0, r)
           for r in range(tb // WINDOW)]
    assert len(mix) == 2 and POST_SPLIT == 4
    (scores0, gmlp0, pv0), (scores1, gmlp1, pv1) = mix
    for piece in (lambda: out_proj(0), lambda: out_proj(1), scores0, lambda: out_proj(2), lambda: out_proj(3),
                  gmlp0, embed, pv0, norm, lambda: gate(0), scores1, lambda: gate(1), gmlp1,
                  lambda: gate(2), pv1, lambda: gate(3), finish):
        piece()


def _layer0_tail(z, h, p, sinks, vnorm, w_sp, b_sp, w_out, w_ple, w_gate, ple_norm, gate_norm, next_norm, tb):
    s, d = h.shape
    n, nb = s // tb, tb // WINDOW
    cur = lambda i: jnp.minimum(i, n - 1)
    old = lambda i: jnp.maximum(i - 1, 0)
    prv = lambda c0: pl.BlockSpec((WINDOW, LANE), lambda i: (jnp.maximum(cur(i) * nb - 1, 0), c0 // LANE))
    full = lambda a: pl.BlockSpec(a.shape, lambda i: (0,) * a.ndim)
    const = lambda shape, r=0: pl.BlockSpec(shape, lambda i: (r, 0), pipeline_mode=pl.Buffered(1))
    per_layer = lambda rows: pl.BlockSpec((None, rows, d), lambda i: (0, 0, 0), pipeline_mode=pl.Buffered(1))
    row_block = pl.BlockSpec((tb, d), lambda i: (old(i), 0))
    vnorm = vnorm.reshape(1, HALF)
    b_t = b_sp.T
    return pl.pallas_call(
        functools.partial(_layer0_tail_kernel, tb=tb),
        grid=(n + 1,),
        in_specs=[pl.BlockSpec(memory_space=pltpu.SMEM),
                  pl.BlockSpec((tb, IN_EVEN), lambda i: (cur(i), 0)), prv(_A_K), prv(_A_V),
                  full(vnorm), full(w_sp), full(b_t),
                  row_block, pl.BlockSpec((None, tb, PLE_DIM), lambda i: (0, old(i), 0)),
                  const((HALF, d), 0), const((HALF, d), 1), per_layer(PLE_DIM), per_layer(d),
                  per_layer(1), per_layer(1), const((1, d))],
        out_specs=(row_block, row_block),
        out_shape=(jax.ShapeDtypeStruct((s, d), F32), jax.ShapeDtypeStruct((s, d), BF16)),
        scratch_shapes=[pltpu.VMEM((2, tb, d), BF16)],
        compiler_params=pltpu.CompilerParams(
            dimension_semantics=("arbitrary",), vmem_limit_bytes=VMEM_LIMIT),
        name="layer0_tail",
    )(sinks, z, z, z, vnorm, w_sp, b_t, h, p, w_out, w_out, w_ple, w_gate, ple_norm, gate_norm,
      next_norm.reshape(1, d))


def _chunk_cumsum(log_g, tb):
    t = lax.broadcasted_iota(jnp.int32, (tb, tb), 0)
    s = lax.broadcasted_iota(jnp.int32, (tb, tb), 1)
    tri = jnp.where((s <= t) & (t // LA_CHUNK == s // LA_CHUNK), 1.0, 0.0).astype(BF16)
    hi = log_g.astype(BF16)
    low = (log_g - hi.astype(F32)).astype(BF16)
    return _dot(tri, hi) + _dot(tri, low)


def _att_exact(qc, g, kbuf, gbuf, r0):
    row = lax.broadcasted_iota(jnp.int32, (LA_CHUNK, LA_CHUNK), 0)
    col = lax.broadcasted_iota(jnp.int32, (LA_CHUNK, LA_CHUNK), 1)
    sub_pos = lax.broadcasted_iota(jnp.int32, (LA_CHUNK, 1), 0) % LA_SUB
    zeros = lambda n: jnp.zeros((n, HEAD_K), F32)
    kc = kbuf[pl.ds(LA_SUB + r0, LA_CHUNK), :]
    att = jnp.zeros((LA_CHUNK, LA_CHUNK), F32)
    for d in range(LA_SUB):
        ks = kbuf[pl.ds(LA_SUB + r0 - d, LA_CHUNK), :]
        gs = gbuf[pl.ds(LA_SUB + r0 - d, LA_CHUNK), :]
        pd = jnp.where(sub_pos >= d, qc * ks * jnp.exp2(g - gs), 0.0)
        att = jnp.where(col == row - d, jnp.sum(pd, axis=-1, keepdims=True), att)
    m = LA_SUB
    while m < LA_CHUNK:
        q_parts, k_parts = [], []
        for b0 in range(0, LA_CHUNK, 2 * m):
            ref_row = g[b0 + m - 1:b0 + m]
            lower, upper = slice(b0, b0 + m), slice(b0 + m, b0 + 2 * m)
            q_parts += [zeros(m), qc[upper] * jnp.exp2(g[upper] - ref_row)]
            k_parts += [kc[lower] * jnp.exp2(ref_row - g[lower]), zeros(m)]
        part = _dot_nt(jnp.concatenate(q_parts, axis=0).astype(BF16),
                       jnp.concatenate(k_parts, axis=0).astype(BF16))
        if 2 * m < LA_CHUNK:
            part = jnp.where(row // (2 * m) == col // (2 * m), part, 0.0)
        att = att + part
        m *= 2
    return att


def _att_factored(qc, kc, g, q_in):
    row = lax.broadcasted_iota(jnp.int32, (LA_CHUNK, LA_CHUNK), 0)
    col = lax.broadcasted_iota(jnp.int32, (LA_CHUNK, LA_CHUNK), 1)
    zeros = lambda n: jnp.zeros((n, HEAD_K), BF16)
    q_slabs, k_slabs = [], []
    for b0 in range(0, LA_CHUNK, LA_SUB):
        blk = slice(b0, b0 + LA_SUB)
        ref_row = g[b0 - 1:b0] if b0 else 0.0
        q_part = (qc[b0:] * jnp.exp2(g[b0:] - ref_row)).astype(BF16) if b0 else q_in
        k_part = (kc[blk] * jnp.exp2(ref_row - g[blk])).astype(BF16)
        q_slabs.append(jnp.concatenate([zeros(b0), q_part], axis=0) if b0 else q_part)
        k_slabs.append(jnp.concatenate(
            [z for z in (zeros(b0), k_part, zeros(LA_CHUNK - LA_SUB - b0)) if z.shape[0]], axis=0))
    att = _dot_nt(jnp.concatenate(q_slabs, axis=1), jnp.concatenate(k_slabs, axis=1))
    return jnp.where(col <= row, att, 0.0)


def _linear_attention_heads(qs, ks, vs, log_g_fns, safe, st_ref, kbuf, gbuf, emit, tb):
    n_heads, n_chunks = len(qs), tb // LA_CHUNK
    chunk = lambda a, c: a[c * LA_CHUNK:(c + 1) * LA_CHUNK]

    def run(exact):
        gcs = [_chunk_cumsum(fn(), tb) for fn in log_g_fns]
        q_in = {(j, c): (chunk(qs[j], c) * jnp.exp2(chunk(gcs[j], c))).astype(BF16)
                for j in range(n_heads) for c in range(n_chunks)}
        att = {}
        for j in range(n_heads):
            if exact:
                kbuf[j, pl.ds(0, LA_SUB), :] = jnp.zeros((LA_SUB, HEAD_K), F32)
                gbuf[j, pl.ds(0, LA_SUB), :] = jnp.zeros((LA_SUB, HEAD_K), F32)
                kbuf[j, pl.ds(LA_SUB, tb), :] = ks[j]
                gbuf[j, pl.ds(LA_SUB, tb), :] = gcs[j]
            for c in range(n_chunks):
                if exact:
                    att[j, c] = _att_exact(chunk(qs[j], c), chunk(gcs[j], c), kbuf.at[j], gbuf.at[j], c * LA_CHUNK)
                else:
                    att[j, c] = _att_factored(chunk(qs[j], c), chunk(ks[j], c), chunk(gcs[j], c), q_in[j, c])
        intra, updates, decays = {}, {}, {}
        for j in range(n_heads):
            for c in range(n_chunks):
                kc, vc, g = chunk(ks[j], c), chunk(vs[j], c), chunk(gcs[j], c)
                g_last = g[LA_CHUNK - 1:LA_CHUNK]
                intra[j, c] = _dot(att[j, c].astype(BF16), vc)
                updates[j, c] = _dot_tn(vc, (kc * jnp.exp2(g_last - g)).astype(BF16))
                decays[j, c] = jnp.exp2(g_last)
        states = {}
        for j in range(n_heads):
            st = st_ref[j]
            for c in range(n_chunks):
                states[j, c] = st.astype(BF16)
                st = st * decays[j, c] + updates[j, c]
            st_ref[j] = st
        for j in range(n_heads):
            emit(j, jnp.concatenate(
                [_dot_nt(q_in[j, c], states[j, c]) + intra[j, c]
                 for c in range(n_chunks)], axis=0))

    pl.when(safe)(lambda: run(exact=False))
    pl.when(jnp.logical_not(safe))(lambda: run(exact=True))


def _la_scratch(tb, dv, hp):
    return [pltpu.VMEM((hp, dv, HEAD_K), F32),
            pltpu.VMEM((hp, LA_SUB + tb, HEAD_K), F32),
            pltpu.VMEM((hp, LA_SUB + tb, HEAD_K), F32)]


def _head_cols(j, width):
    return slice(j * width, (j + 1) * width)


def _gated_head_norm(y_ref, on_ref, gate_ref, width):
    def emit(j, o):
        cols = _head_cols(j, width)
        y_ref[:, cols] = (_rms(o, on_ref[:, cols]) * _silu(gate_ref[:, cols].astype(F32))).astype(BF16)
    return emit


def _hgrn2_kernel(q_ref, f_ref, i_ref, gate_ref, lb_ref, on_ref, y_ref, st_ref, kbuf, gbuf, *, tb, hp):
    @pl.when(pl.program_id(1) == 0)
    def _():
        st_ref[...] = jnp.zeros_like(st_ref)

    qs, ks, vs, fs = [], [], [], []
    for j in range(hp):
        cols = _head_cols(j, HEAD_K)
        lb = lb_ref[:, cols]
        f = lb + (1.0 - lb) * jax.nn.sigmoid(f_ref[:, cols].astype(F32))
        qs.append(q_ref[:, cols].astype(F32) * (HEAD_K ** -0.5))
        ks.append(1.0 - f)
        vs.append(i_ref[:, cols])
        fs.append(f)
    safe = jnp.min(functools.reduce(jnp.minimum, fs)) >= math.exp(-MAX_SAFE_STEP_DECAY)
    _linear_attention_heads(qs, ks, vs, [functools.partial(jnp.log2, f) for f in fs], safe, st_ref, kbuf, gbuf,
                            _gated_head_norm(y_ref, on_ref, gate_ref, HEAD_V_C), tb)


def _gla_kernel(q_ref, k_ref, v_ref, gate_ref, glr_ref, wup_ref, bg_ref, on_ref, y_ref,
                st_ref, kbuf, gbuf, *, tb, hp):
    @pl.when(pl.program_id(1) == 0)
    def _():
        st_ref[...] = jnp.zeros_like(st_ref)

    x = _dot(glr_ref[...], wup_ref[...]) + bg_ref[...]
    log_sigmoid = lambda t: -(jnp.maximum(-t, 0.0) + jnp.log1p(jnp.exp(-jnp.abs(t))))
    to_log2 = math.log2(math.e) / GATE_LOGIT_NORM_D
    qs, ks, vs, lgs = [], [], [], []
    for j in range(hp):
        cols = _head_cols(j, HEAD_K)
        qs.append(q_ref[:, cols].astype(F32) * (HEAD_K ** -0.5))
        ks.append(k_ref[:, cols].astype(F32))
        vs.append(v_ref[:, _head_cols(j, HEAD_V_D)])
        lgs.append(lambda cols=cols: log_sigmoid(x[:, cols]) * to_log2)
    safe = jnp.max(-x) <= MAX_SAFE_STEP_DECAY * GATE_LOGIT_NORM_D - math.log(2.0)
    _linear_attention_heads(qs, ks, vs, lgs, safe, st_ref, kbuf, gbuf,
                            _gated_head_norm(y_ref, on_ref, gate_ref, HEAD_V_D), tb)


_C_Q, _C_F, _C_I, _C_GATE = 0, 8, 16, 24
_D_Q, _D_K, _D_V, _D_GATE, _D_GLR = 32, 36, 40, 48, 56
PACKED_ODD = 57 * LANE


def _hgrn2(z, lb, onorm, tb, hp):
    s = z.shape[0]
    kw, vw = hp * HEAD_K, hp * HEAD_V_C
    col = lambda c0: pl.BlockSpec((tb, kw), lambda h, i: (i, c0 // hp + h))
    vec = lambda w: pl.BlockSpec((1, w), lambda h, i: (0, h))
    return pl.pallas_call(
        functools.partial(_hgrn2_kernel, tb=tb, hp=hp),
        grid=(N_HEADS_C // hp, s // tb),
        in_specs=[col(_C_Q), col(_C_F), col(_C_I), col(_C_GATE), vec(kw), vec(vw)],
        out_specs=pl.BlockSpec((tb, vw), lambda h, i: (i, h)),
        out_shape=jax.ShapeDtypeStruct((s, HALF), BF16),
        scratch_shapes=_la_scratch(tb, HEAD_V_C, hp),
        compiler_params=pltpu.CompilerParams(
            dimension_semantics=("arbitrary", "arbitrary"), vmem_limit_bytes=VMEM_LIMIT),
        name="hgrn2",
    )(z, z, z, z, lb.reshape(1, HALF), onorm.reshape(1, HALF))


def _gla(z, w_up, b_gate, onorm, tb, hp):
    s = z.shape[0]
    kw, vw = hp * HEAD_K, hp * HEAD_V_D
    col = lambda c0, w: pl.BlockSpec((tb, w), lambda h, i: (i, c0 * LANE // w + h))
    vec = lambda w: pl.BlockSpec((1, w), lambda h, i: (0, h))
    w_up = jnp.pad(w_up, ((0, LANE - GATE_RANK_D), (0, 0))).astype(BF16)
    return pl.pallas_call(
        functools.partial(_gla_kernel, tb=tb, hp=hp),
        grid=(N_HEADS_D // hp, s // tb),
        in_specs=[col(_D_Q, kw), col(_D_K, kw), col(_D_V, vw), col(_D_GATE, vw),
                  pl.BlockSpec((tb, LANE), lambda h, i: (i, _D_GLR)),
                  pl.BlockSpec((LANE, kw), lambda h, i: (0, h)),
                  vec(kw), vec(vw)],
        out_specs=pl.BlockSpec((tb, vw), lambda h, i: (i, h)),
        out_shape=jax.ShapeDtypeStruct((s, HALF), BF16),
        scratch_shapes=_la_scratch(tb, HEAD_V_D, hp),
        compiler_params=pltpu.CompilerParams(
            dimension_semantics=("arbitrary", "arbitrary"), vmem_limit_bytes=VMEM_LIMIT),
        name="gla",
    )(z, z, z, z, z, w_up, b_gate.reshape(1, N_HEADS_D * HEAD_K), onorm.reshape(1, HALF))


def _post_kernel(ya_ref, yb_ref, h_ref, p_ref, woa_ref, wob_ref, wple_ref, wg_ref,
                 pn_ref, gn_ref, nn_ref, *out_refs, final):
    out_proj, embed, norm, gate, finish = _post_phases(
        lambda: ya_ref[...], lambda: yb_ref[...], h_ref, p_ref, woa_ref, wob_ref, wple_ref, wg_ref,
        pn_ref, gn_ref, nn_ref, out_refs, final)
    for c in range(POST_SPLIT):
        out_proj(c)
    embed()
    norm()
    for c in range(POST_SPLIT):
        gate(c)
    finish()


def _post(ya, yb, ca, cb, h, p, layer, w_out, w_ple, w_gate, ple_norm, gate_norm, next_norm, tm, final):
    s, d = h.shape
    const = lambda shape, r=0: pl.BlockSpec(shape, lambda i: (r, 0), pipeline_mode=pl.Buffered(1))
    per_layer = lambda rows: pl.BlockSpec((None, rows, d), lambda i: (layer, 0, 0), pipeline_mode=pl.Buffered(1))
    row_block = pl.BlockSpec((tm, d), lambda i: (i, 0))
    out_f32, out_bf16 = jax.ShapeDtypeStruct((s, d), F32), jax.ShapeDtypeStruct((s, d), BF16)
    return pl.pallas_call(
        functools.partial(_post_kernel, final=final),
        grid=(s // tm,),
        in_specs=[pl.BlockSpec((tm, HALF), lambda i: (i, ca)),
                  pl.BlockSpec((tm, HALF), lambda i: (i, cb)),
                  row_block,
                  pl.BlockSpec((None, tm, PLE_DIM), lambda i: (layer, i, 0)),
                  const((HALF, d), 0), const((HALF, d), 1), per_layer(PLE_DIM), per_layer(d),
                  per_layer(1), per_layer(1), const((1, d))],
        out_specs=row_block if final else (row_block, row_block),
        out_shape=out_f32 if final else (out_f32, out_bf16),
        compiler_params=pltpu.CompilerParams(
            dimension_semantics=("arbitrary",), vmem_limit_bytes=VMEM_LIMIT),
        name="post_final" if final else "post",
    )(ya, yb, h, p, w_out, w_out, w_ple, w_gate, ple_norm, gate_norm, next_norm.reshape(1, d))


def _pack_odd_kernel(a_ref, b_ref, o_ref):
    j = pl.program_id(0)

    def emit(t):
        o_ref[...] = t.T.astype(BF16)

    @pl.when(j < _D_GATE)
    def _():
        emit(a_ref[...])

    @pl.when((j >= _D_GATE) & (j < _D_GLR))
    def _():
        emit(jnp.concatenate([a_ref[GATE_RANK_D:, :], b_ref[:GATE_RANK_D, :]], axis=0))

    @pl.when(j == _D_GLR)
    def _():
        emit(jnp.concatenate(
            [a_ref[:GATE_RANK_D, :], jnp.zeros((LANE - GATE_RANK_D, a_ref.shape[1]), F32)], axis=0))


def _pack_odd(w_t):
    n, d = w_t.shape
    n_blocks = PACKED_ODD // LANE
    src_a = lambda j: jnp.where(j == _D_GLR, _D_GATE, j)
    src_b = lambda j: jnp.clip(j + 1, _D_GATE + 1, pl.cdiv(n, LANE) - 1)
    return pl.pallas_call(
        _pack_odd_kernel,
        grid=(n_blocks,),
        in_specs=[pl.BlockSpec((LANE, d), lambda j: (src_a(j), 0)),
                  pl.BlockSpec((LANE, d), lambda j: (src_b(j), 0))],
        out_specs=pl.BlockSpec((d, LANE), lambda j: (0, j)),
        out_shape=jax.ShapeDtypeStruct((d, PACKED_ODD), BF16),
        compiler_params=pltpu.CompilerParams(
            dimension_semantics=("arbitrary",), vmem_limit_bytes=VMEM_LIMIT),
        name="pack_odd",
    )(w_t, w_t)


def kernel(x, p, norm_mix, w_in_even, sinks_a, vnorm_b, w_spatial_b, b_spatial_b, w_out_even,
           w_in_odd, lower_bounds_c, onorm_c, w_gate_up_d, b_gate_d, onorm_d, w_out_odd,
           w_ple_proj, ple_norm, ple_gate_norm, w_ple_gate, final_norm):
    assert x.shape[0] == 1 and x.shape[2] == D_MODEL and norm_mix.shape[0] == 2
    s = x.shape[1]
    h = x[0]
    p = p.reshape(2, s, PLE_DIM)
    w_ple, w_gate = w_ple_proj.astype(BF16), w_ple_gate.astype(BF16)
    ple_norm, gate_norm = ple_norm.reshape(2, 1, D_MODEL), ple_gate_norm.reshape(2, 1, D_MODEL)
    post = functools.partial(_post, p=p, w_ple=w_ple, w_gate=w_gate, ple_norm=ple_norm, gate_norm=gate_norm,
                             tm=256)

    z = _norm_matmul(h, norm_mix[0], w_in_even[0].astype(BF16), tm=min(s, 1024), tn=IN_EVEN // 3)
    h, hn = _layer0_tail(z, h, p, sinks_a[0], vnorm_b[0], w_spatial_b[0], b_spatial_b[0],
                         w_out_even[0].astype(BF16), w_ple, w_gate, ple_norm, gate_norm, norm_mix[1], tb=256)

    z = _norm_matmul(hn, None, _pack_odd(w_in_odd[0].T), tm=min(s, 1024), tn=PACKED_ODD // 3)
    sm = jax.nn.softmax(lower_bounds_c.astype(F32), axis=0)
    lb = (jnp.cumsum(sm, axis=0) - sm[0])[1]
    yc = _hgrn2(z, lb, onorm_c[0], tb=256, hp=4)
    yd = _gla(z, w_gate_up_d[0], b_gate_d[0], onorm_d[0], tb=256, hp=2)
    out = post(yc, yd, 0, 0, h, layer=1, w_out=w_out_odd[0].astype(BF16), next_norm=final_norm, final=True)
    return out[None]
```

```python
import functools
import math

import jax
import jax.numpy as jnp
from jax import lax
from jax.experimental import pallas as pl
from jax.experimental.pallas import tpu as pltpu

F32 = jnp.float32
BF16 = jnp.bfloat16

EPS = 1e-6
NEG_INF = -1e30

D_MODEL = 2048
PLE_DIM = 256
HALF = D_MODEL // 2
HEAD_A = 64
N_KV_A = 2
Q_PER_KV_A = 8
WINDOW = 128
N_GROUPS_B = 8
HEAD_K = 128
N_HEADS_C = 8
N_HEADS_D = 4
HEAD_V_C = 128
HEAD_V_D = 256
GATE_RANK_D = 16
GATE_LOGIT_NORM_D = 16.0
LA_CHUNK = 64
LA_SUB = 16
MAX_SAFE_STEP_DECAY = 60.0 / LA_SUB
LANE = 128
VMEM_LIMIT = 56 * 1024 * 1024


def _dot(a, b):
    return jnp.dot(a, b, preferred_element_type=F32)


def _dot_nt(a, b):
    return lax.dot_general(a, b, (((1,), (1,)), ((), ())), preferred_element_type=F32)


def _dot_tn(a, b):
    return lax.dot_general(a, b, (((0,), (0,)), ((), ())), preferred_element_type=F32)


def _rms(x, gain):
    return x * lax.rsqrt(jnp.mean(x * x, axis=-1, keepdims=True) + EPS) * gain


def _silu(x):
    return x * jax.nn.sigmoid(x)


def _gelu(x):
    return 0.5 * x * (1.0 + lax.erf(x * (2.0 ** -0.5)))


def _norm_matmul_kernel(x_ref, g_ref, w_ref, o_ref, xn_ref):
    @pl.when(pl.program_id(1) == 0)
    def _():
        xn_ref[...] = _rms(x_ref[...], g_ref[...]).astype(BF16)

    o_ref[...] = _dot(xn_ref[...], w_ref[...]).astype(o_ref.dtype)


def _matmul_kernel(x_ref, w_ref, o_ref):
    o_ref[...] = _dot(x_ref[...], w_ref[...]).astype(o_ref.dtype)


def _norm_matmul(x, gain, w, tm, tn):
    s, d = x.shape
    n = w.shape[1]
    x_spec = pl.BlockSpec((tm, d), lambda i, j: (i, 0))
    w_spec = pl.BlockSpec((d, tn), lambda i, j: (0, j))
    common = dict(
        grid=(s // tm, n // tn),
        out_specs=pl.BlockSpec((tm, tn), lambda i, j: (i, j)),
        out_shape=jax.ShapeDtypeStruct((s, n), BF16),
        compiler_params=pltpu.CompilerParams(
            dimension_semantics=("arbitrary", "arbitrary"), vmem_limit_bytes=VMEM_LIMIT))
    if gain is None:
        return pl.pallas_call(_matmul_kernel, in_specs=[x_spec, w_spec], name="matmul", **common)(x, w)
    return pl.pallas_call(
        _norm_matmul_kernel,
        in_specs=[x_spec, pl.BlockSpec((1, d), lambda i, j: (0, 0)), w_spec],
        scratch_shapes=[pltpu.VMEM((tm, d), BF16)],
        name="norm_matmul", **common)(x, gain.reshape(1, d), w)


_A_Q, _A_K, _A_V, _A_GATE, _B_U, _B_V, _B_GATE = 0, 1024, 1152, 1280, 2304, 3328, 4352
IN_EVEN = 5376


def _mixer0_phases(sinks_ref, z_ref, kp_ref, vp_ref, vn_ref, wsp_ref, bsp_ref, y_ref, first, r):
    lane = lax.broadcasted_iota(jnp.int32, (1, LANE), 1)
    lo = lane < HEAD_A
    qi = lax.broadcasted_iota(jnp.int32, (WINDOW, 2 * WINDOW), 0)
    kj = lax.broadcasted_iota(jnp.int32, (WINDOW, 2 * WINDOW), 1)
    band = (kj > qi) & (kj <= qi + WINDOW)
    tri = (lax.broadcasted_iota(jnp.int32, (WINDOW, WINDOW), 0)
           >= lax.broadcasted_iota(jnp.int32, (WINDOW, WINDOW), 1))
    zero_bf = jnp.zeros((), BF16)
    n_q = Q_PER_KV_A
    rows = pl.ds(r * WINDOW, WINDOW)
    zcols = lambda c0, w: z_ref[rows, pl.ds(c0, w)]
    st = {}

    def swap_halves(t):
        return jnp.concatenate([t[:, HEAD_A:], t[:, :HEAD_A]], axis=1)

    def both_halves(t, t_sw, h):
        return jnp.where(lo, t, t_sw) if h == 0 else jnp.where(lo, t_sw, t)

    def scores():
        if r == 0:
            k_prev, v_prev = kp_ref[...], vp_ref[...]
            st["mask"] = band & ((kj >= WINDOW) | jnp.logical_not(first))
        else:
            prev = pl.ds((r - 1) * WINDOW, WINDOW)
            k_prev, v_prev = z_ref[prev, pl.ds(_A_K, LANE)], z_ref[prev, pl.ds(_A_V, LANE)]
            st["mask"] = band
        kk = jnp.concatenate([k_prev, zcols(_A_K, LANE)], axis=0)
        vv = jnp.concatenate([v_prev, zcols(_A_V, LANE)], axis=0)
        kk_sw = swap_halves(kk)
        st["vv"], st["vv_sw"] = vv, swap_halves(vv)
        st["scores"] = []
        for h in range(N_KV_A):
            parts = []
            for j in range(n_q // 2):
                qp = zcols(_A_Q + (h * n_q + 2 * j) * HEAD_A, LANE) * jnp.asarray(HEAD_A ** -0.5, BF16)
                parts += [jnp.where(lo, qp, zero_bf), jnp.where(lo, zero_bf, qp)]
            st["scores"].append(_dot_nt(jnp.concatenate(parts, axis=0), both_halves(kk, kk_sw, h)))

    def gmlp():
        u = _gelu(zcols(_B_U, HALF).astype(F32))
        vg = _rms(_gelu(zcols(_B_V, HALF).astype(F32)), vn_ref[...]).astype(BF16)
        gb = zcols(_B_GATE, HALF).astype(F32)
        for g in range(N_GROUPS_B):
            cols = slice(g * LANE, (g + 1) * LANE)
            w = jnp.where(tri, wsp_ref[g], 0.0).astype(BF16)
            mixed = _dot(w, vg[:, cols]) + bsp_ref[:, g:g + 1]
            y_ref[rows, pl.ds(HALF + g * LANE, LANE)] = (u[:, cols] * mixed * _silu(gb[:, cols])).astype(BF16)

    def softmax_pv():
        mask, vv, vv_sw = st["mask"], st["vv"], st["vv_sw"]
        for h in range(N_KV_A):
            ps, rs = [], []
            for i in range(n_q):
                sink = sinks_ref[h * n_q + i]
                sc = jnp.where(mask, st["scores"][h][i * WINDOW:(i + 1) * WINDOW], NEG_INF)
                m = jnp.maximum(jnp.max(sc, axis=-1, keepdims=True), sink)
                p = jnp.exp(sc - m)
                rs.append(1.0 / (jnp.sum(p, axis=-1, keepdims=True) + jnp.exp(sink - m)))
                ps.append(p.astype(BF16))
            v_both = both_halves(vv, vv_sw, h)
            v2 = jnp.concatenate([jnp.where(lo, v_both, zero_bf), jnp.where(lo, zero_bf, v_both)], axis=0)
            p_pairs = jnp.concatenate(
                [jnp.concatenate(ps[2 * j:2 * j + 2], axis=1) for j in range(n_q // 2)], axis=0)
            o = _dot(p_pairs, v2)
            for j in range(n_q // 2):
                c0 = (h * n_q + 2 * j) * HEAD_A
                oj = o[j * WINDOW:(j + 1) * WINDOW] * jnp.where(lo, rs[2 * j], rs[2 * j + 1])
                g = zcols(_A_GATE + c0, LANE).astype(F32)
                y_ref[rows, pl.ds(c0, LANE)] = (oj * _silu(g)).astype(BF16)

    return scores, gmlp, softmax_pv


POST_SPLIT = 4


def _post_phases(ya, yb, h_ref, p_ref, woa_ref, wob_ref, wple_ref, wg_ref, pn_ref, gn_ref, nn_ref, out_refs, final):
    cw = D_MODEL // POST_SPLIT
    st = {"h1": [], "gate": []}

    def out_proj(c):
        cols = pl.ds(c * cw, cw)
        st["h1"].append(h_ref[:, cols] + _dot(ya(), woa_ref[:, cols]) + _dot(yb(), wob_ref[:, cols]))

    def embed():
        st["e"] = _rms(_dot(p_ref[...].astype(BF16), wple_ref[...]), pn_ref[...])

    def norm():
        st["h1"] = jnp.concatenate(st["h1"], axis=1)
        st["hn1"] = _rms(st["h1"], gn_ref[...]).astype(BF16)

    def gate(c):
        st["gate"].append(jax.nn.sigmoid(_dot(st["hn1"], wg_ref[:, pl.ds(c * cw, cw)])))

    def finish():
        h2 = st["h1"] + st["e"] * jnp.concatenate(st["gate"], axis=1)
        normed = _rms(h2, nn_ref[...])
        if final:
            out_refs[0][...] = normed
        else:
            out_refs[0][...] = h2
            out_refs[1][...] = normed.astype(BF16)

    return out_proj, embed, norm, gate, finish


def _layer0_tail_kernel(sinks_ref, z_ref, kp_ref, vp_ref, vn_ref, wsp_ref, bsp_ref,
                        h_ref, p_ref, woa_ref, wob_ref, wple_ref, wg_ref, pn_ref, gn_ref, nn_ref,
                        h_out_ref, hn_out_ref, y_buf, *, tb):
    i = pl.program_id(0)

    @pl.when(i == 0)
    def _():
        y_buf[...] = jnp.zeros_like(y_buf)

    slot = lax.rem(i, 2)
    y_new, y_old = y_buf.at[slot], y_buf.at[1 - slot]
    out_proj, embed, norm, gate, finish = _post_phases(
        lambda: y_old[:, :HALF], lambda: y_old[:, HALF:], h_ref, p_ref, woa_ref, wob_ref, wple_ref, wg_ref,
        pn_ref, gn_ref, nn_ref, (h_out_ref, hn_out_ref), final=False)
    mix = [_mixer0_phases(sinks_ref, z_ref, kp_ref, vp_ref, vn_ref, wsp_ref, bsp_ref, y_new, i == 0, r)
           for r in range(tb // WINDOW)]
    assert len(mix) == 2 and POST_SPLIT == 4
    (scores0, gmlp0, pv0), (scores1, gmlp1, pv1) = mix
    for piece in (lambda: out_proj(0), lambda: out_proj(1), scores0, lambda: out_proj(2), lambda: out_proj(3),
                  gmlp0, embed, pv0, norm, lambda: gate(0), scores1, lambda: gate(1), gmlp1,
                  lambda: gate(2), pv1, lambda: gate(3), finish):
        piece()


def _layer0_tail(z, h, p, sinks, vnorm, w_sp, b_sp, w_out, w_ple, w_gate, ple_norm, gate_norm, next_norm, tb):
    s, d = h.shape
    n, nb = s // tb, tb // WINDOW
    cur = lambda i: jnp.minimum(i, n - 1)
    old = lambda i: jnp.maximum(i - 1, 0)
    prv = lambda c0: pl.BlockSpec((WINDOW, LANE), lambda i: (jnp.maximum(cur(i) * nb - 1, 0), c0 // LANE))
    full = lambda a: pl.BlockSpec(a.shape, lambda i: (0,) * a.ndim)
    const = lambda shape, r=0: pl.BlockSpec(shape, lambda i: (r, 0), pipeline_mode=pl.Buffered(1))
    per_layer = lambda rows: pl.BlockSpec((None, rows, d), lambda i: (0, 0, 0), pipeline_mode=pl.Buffered(1))
    row_block = pl.BlockSpec((tb, d), lambda i: (old(i), 0))
    vnorm = vnorm.reshape(1, HALF)
    b_t = b_sp.T
    return pl.pallas_call(
        functools.partial(_layer0_tail_kernel, tb=tb),
        grid=(n + 1,),
        in_specs=[pl.BlockSpec(memory_space=pltpu.SMEM),
                  pl.BlockSpec((tb, IN_EVEN), lambda i: (cur(i), 0)), prv(_A_K), prv(_A_V),
                  full(vnorm), full(w_sp), full(b_t),
                  row_block, pl.BlockSpec((None, tb, PLE_DIM), lambda i: (0, old(i), 0)),
                  const((HALF, d), 0), const((HALF, d), 1), per_layer(PLE_DIM), per_layer(d),
                  per_layer(1), per_layer(1), const((1, d))],
        out_specs=(row_block, row_block),
        out_shape=(jax.ShapeDtypeStruct((s, d), F32), jax.ShapeDtypeStruct((s, d), BF16)),
        scratch_shapes=[pltpu.VMEM((2, tb, d), BF16)],
        compiler_params=pltpu.CompilerParams(
            dimension_semantics=("arbitrary",), vmem_limit_bytes=VMEM_LIMIT),
        name="layer0_tail",
    )(sinks, z, z, z, vnorm, w_sp, b_t, h, p, w_out, w_out, w_ple, w_gate, ple_norm, gate_norm,
      next_norm.reshape(1, d))


def _chunk_cumsum(log_g, tb):
    t = lax.broadcasted_iota(jnp.int32, (tb, tb), 0)
    s = lax.broadcasted_iota(jnp.int32, (tb, tb), 1)
    tri = jnp.where((s <= t) & (t // LA_CHUNK == s // LA_CHUNK), 1.0, 0.0).astype(BF16)
    hi = log_g.astype(BF16)
    low = (log_g - hi.astype(F32)).astype(BF16)
    return _dot(tri, hi) + _dot(tri, low)


def _att_exact(qc, g, kbuf, gbuf, r0):
    row = lax.broadcasted_iota(jnp.int32, (LA_CHUNK, LA_CHUNK), 0)
    col = lax.broadcasted_iota(jnp.int32, (LA_CHUNK, LA_CHUNK), 1)
    sub_pos = lax.broadcasted_iota(jnp.int32, (LA_CHUNK, 1), 0) % LA_SUB
    zeros = lambda n: jnp.zeros((n, HEAD_K), F32)
    kc = kbuf[pl.ds(LA_SUB + r0, LA_CHUNK), :]
    att = jnp.zeros((LA_CHUNK, LA_CHUNK), F32)
    for d in range(LA_SUB):
        ks = kbuf[pl.ds(LA_SUB + r0 - d, LA_CHUNK), :]
        gs = gbuf[pl.ds(LA_SUB + r0 - d, LA_CHUNK), :]
        pd = jnp.where(sub_pos >= d, qc * ks * jnp.exp2(g - gs), 0.0)
        att = jnp.where(col == row - d, jnp.sum(pd, axis=-1, keepdims=True), att)
    m = LA_SUB
    while m < LA_CHUNK:
        q_parts, k_parts = [], []
        for b0 in range(0, LA_CHUNK, 2 * m):
            ref_row = g[b0 + m - 1:b0 + m]
            lower, upper = slice(b0, b0 + m), slice(b0 + m, b0 + 2 * m)
            q_parts += [zeros(m), qc[upper] * jnp.exp2(g[upper] - ref_row)]
            k_parts += [kc[lower] * jnp.exp2(ref_row - g[lower]), zeros(m)]
        part = _dot_nt(jnp.concatenate(q_parts, axis=0).astype(BF16),
                       jnp.concatenate(k_parts, axis=0).astype(BF16))
        if 2 * m < LA_CHUNK:
            part = jnp.where(row // (2 * m) == col // (2 * m), part, 0.0)
        att = att + part
        m *= 2
    return att


def _att_factored(qc, kc, g, q_in):
    row = lax.broadcasted_iota(jnp.int32, (LA_CHUNK, LA_CHUNK), 0)
    col = lax.broadcasted_iota(jnp.int32, (LA_CHUNK, LA_CHUNK), 1)
    zeros = lambda n: jnp.zeros((n, HEAD_K), BF16)
    q_slabs, k_slabs = [], []
    for b0 in range(0, LA_CHUNK, LA_SUB):
        blk = slice(b0, b0 + LA_SUB)
        ref_row = g[b0 - 1:b0] if b0 else 0.0
        q_part = (qc[b0:] * jnp.exp2(g[b0:] - ref_row)).astype(BF16) if b0 else q_in
        k_part = (kc[blk] * jnp.exp2(ref_row - g[blk])).astype(BF16)
        q_slabs.append(jnp.concatenate([zeros(b0), q_part], axis=0) if b0 else q_part)
        k_slabs.append(jnp.concatenate(
            [z for z in (zeros(b0), k_part, zeros(LA_CHUNK - LA_SUB - b0)) if z.shape[0]], axis=0))
    att = _dot_nt(jnp.concatenate(q_slabs, axis=1), jnp.concatenate(k_slabs, axis=1))
    return jnp.where(col <= row, att, 0.0)


def _linear_attention_heads(qs, ks, vs, log_g_fns, safe, st_ref, kbuf, gbuf, emit, tb):
    n_heads, n_chunks = len(qs), tb // LA_CHUNK
    chunk = lambda a, c: a[c * LA_CHUNK:(c + 1) * LA_CHUNK]

    def run(exact):
        gcs = [_chunk_cumsum(fn(), tb) for fn in log_g_fns]
        q_in = {(j, c): (chunk(qs[j], c) * jnp.exp2(chunk(gcs[j], c))).astype(BF16)
                for j in range(n_heads) for c in range(n_chunks)}
        att = {}
        for j in range(n_heads):
            if exact:
                kbuf[j, pl.ds(0, LA_SUB), :] = jnp.zeros((LA_SUB, HEAD_K), F32)
                gbuf[j, pl.ds(0, LA_SUB), :] = jnp.zeros((LA_SUB, HEAD_K), F32)
                kbuf[j, pl.ds(LA_SUB, tb), :] = ks[j]
                gbuf[j, pl.ds(LA_SUB, tb), :] = gcs[j]
            for c in range(n_chunks):
                if exact:
                    att[j, c] = _att_exact(chunk(qs[j], c), chunk(gcs[j], c), kbuf.at[j], gbuf.at[j], c * LA_CHUNK)
                else:
                    att[j, c] = _att_factored(chunk(qs[j], c), chunk(ks[j], c), chunk(gcs[j], c), q_in[j, c])
        intra, updates, decays = {}, {}, {}
        for j in range(n_heads):
            for c in range(n_chunks):
                kc, vc, g = chunk(ks[j], c), chunk(vs[j], c), chunk(gcs[j], c)
                g_last = g[LA_CHUNK - 1:LA_CHUNK]
                intra[j, c] = _dot(att[j, c].astype(BF16), vc)
                updates[j, c] = _dot_tn(vc, (kc * jnp.exp2(g_last - g)).astype(BF16))
                decays[j, c] = jnp.exp2(g_last)
        states = {}
        for j in range(n_heads):
            st = st_ref[j]
            for c in range(n_chunks):
                states[j, c] = st.astype(BF16)
                st = st * decays[j, c] + updates[j, c]
            st_ref[j] = st
        for j in range(n_heads):
            emit(j, jnp.concatenate(
                [_dot_nt(q_in[j, c], states[j, c]) + intra[j, c]
                 for c in range(n_chunks)], axis=0))

    pl.when(safe)(lambda: run(exact=False))
    pl.when(jnp.logical_not(safe))(lambda: run(exact=True))


def _la_scratch(tb, dv, hp):
    return [pltpu.VMEM((hp, dv, HEAD_K), F32),
            pltpu.VMEM((hp, LA_SUB + tb, HEAD_K), F32),
            pltpu.VMEM((hp, LA_SUB + tb, HEAD_K), F32)]


def _head_cols(j, width):
    return slice(j * width, (j + 1) * width)


def _gated_head_norm(y_ref, on_ref, gate_ref, width):
    def emit(j, o):
        cols = _head_cols(j, width)
        y_ref[:, cols] = (_rms(o, on_ref[:, cols]) * _silu(gate_ref[:, cols].astype(F32))).astype(BF16)
    return emit


def _hgrn2_kernel(q_ref, f_ref, i_ref, gate_ref, lb_ref, on_ref, y_ref, st_ref, kbuf, gbuf, *, tb, hp):
    @pl.when(pl.program_id(1) == 0)
    def _():
        st_ref[...] = jnp.zeros_like(st_ref)

    qs, ks, vs, fs = [], [], [], []
    for j in range(hp):
        cols = _head_cols(j, HEAD_K)
        lb = lb_ref[:, cols]
        f = lb + (1.0 - lb) * jax.nn.sigmoid(f_ref[:, cols].astype(F32))
        qs.append(q_ref[:, cols].astype(F32) * (HEAD_K ** -0.5))
        ks.append(1.0 - f)
        vs.append(i_ref[:, cols])
        fs.append(f)
    safe = jnp.min(functools.reduce(jnp.minimum, fs)) >= math.exp(-MAX_SAFE_STEP_DECAY)
    _linear_attention_heads(qs, ks, vs, [functools.partial(jnp.log2, f) for f in fs], safe, st_ref, kbuf, gbuf,
                            _gated_head_norm(y_ref, on_ref, gate_ref, HEAD_V_C), tb)


def _gla_kernel(q_ref, k_ref, v_ref, gate_ref, glr_ref, wup_ref, bg_ref, on_ref, y_ref,
                st_ref, kbuf, gbuf, *, tb, hp):
    @pl.when(pl.program_id(1) == 0)
    def _():
        st_ref[...] = jnp.zeros_like(st_ref)

    x = _dot(glr_ref[...], wup_ref[...]) + bg_ref[...]
    log_sigmoid = lambda t: -(jnp.maximum(-t, 0.0) + jnp.log1p(jnp.exp(-jnp.abs(t))))
    to_log2 = math.log2(math.e) / GATE_LOGIT_NORM_D
    qs, ks, vs, lgs = [], [], [], []
    for j in range(hp):
        cols = _head_cols(j, HEAD_K)
        qs.append(q_ref[:, cols].astype(F32) * (HEAD_K ** -0.5))
        ks.append(k_ref[:, cols].astype(F32))
        vs.append(v_ref[:, _head_cols(j, HEAD_V_D)])
        lgs.append(lambda cols=cols: log_sigmoid(x[:, cols]) * to_log2)
    safe = jnp.max(-x) <= MAX_SAFE_STEP_DECAY * GATE_LOGIT_NORM_D - math.log(2.0)
    _linear_attention_heads(qs, ks, vs, lgs, safe, st_ref, kbuf, gbuf,
                            _gated_head_norm(y_ref, on_ref, gate_ref, HEAD_V_D), tb)


_C_Q, _C_F, _C_I, _C_GATE = 0, 8, 16, 24
_D_Q, _D_K, _D_V, _D_GATE, _D_GLR = 32, 36, 40, 48, 56
PACKED_ODD = 57 * LANE


def _hgrn2(z, lb, onorm, tb, hp):
    s = z.shape[0]
    kw, vw = hp * HEAD_K, hp * HEAD_V_C
    col = lambda c0: pl.BlockSpec((tb, kw), lambda h, i: (i, c0 // hp + h))
    vec = lambda w: pl.BlockSpec((1, w), lambda h, i: (0, h))
    return pl.pallas_call(
        functools.partial(_hgrn2_kernel, tb=tb, hp=hp),
        grid=(N_HEADS_C // hp, s // tb),
        in_specs=[col(_C_Q), col(_C_F), col(_C_I), col(_C_GATE), vec(kw), vec(vw)],
        out_specs=pl.BlockSpec((tb, vw), lambda h, i: (i, h)),
        out_shape=jax.ShapeDtypeStruct((s, HALF), BF16),
        scratch_shapes=_la_scratch(tb, HEAD_V_C, hp),
        compiler_params=pltpu.CompilerParams(
            dimension_semantics=("arbitrary", "arbitrary"), vmem_limit_bytes=VMEM_LIMIT),
        name="hgrn2",
    )(z, z, z, z, lb.reshape(1, HALF), onorm.reshape(1, HALF))


def _gla(z, w_up, b_gate, onorm, tb, hp):
    s = z.shape[0]
    kw, vw = hp * HEAD_K, hp * HEAD_V_D
    col = lambda c0, w: pl.BlockSpec((tb, w), lambda h, i: (i, c0 * LANE // w + h))
    vec = lambda w: pl.BlockSpec((1, w), lambda h, i: (0, h))
    w_up = jnp.pad(w_up, ((0, LANE - GATE_RANK_D), (0, 0))).astype(BF16)
    return pl.pallas_call(
        functools.partial(_gla_kernel, tb=tb, hp=hp),
        grid=(N_HEADS_D // hp, s // tb),
        in_specs=[col(_D_Q, kw), col(_D_K, kw), col(_D_V, vw), col(_D_GATE, vw),
                  pl.BlockSpec((tb, LANE), lambda h, i: (i, _D_GLR)),
                  pl.BlockSpec((LANE, kw), lambda h, i: (0, h)),
                  vec(kw), vec(vw)],
        out_specs=pl.BlockSpec((tb, vw), lambda h, i: (i, h)),
        out_shape=jax.ShapeDtypeStruct((s, HALF), BF16),
        scratch_shapes=_la_scratch(tb, HEAD_V_D, hp),
        compiler_params=pltpu.CompilerParams(
            dimension_semantics=("arbitrary", "arbitrary"), vmem_limit_bytes=VMEM_LIMIT),
        name="gla",
    )(z, z, z, z, z, w_up, b_gate.reshape(1, N_HEADS_D * HEAD_K), onorm.reshape(1, HALF))


def _post_kernel(ya_ref, yb_ref, h_ref, p_ref, woa_ref, wob_ref, wple_ref, wg_ref,
                 pn_ref, gn_ref, nn_ref, *out_refs, final):
    out_proj, embed, norm, gate, finish = _post_phases(
        lambda: ya_ref[...], lambda: yb_ref[...], h_ref, p_ref, woa_ref, wob_ref, wple_ref, wg_ref,
        pn_ref, gn_ref, nn_ref, out_refs, final)
    for c in range(POST_SPLIT):
        out_proj(c)
    embed()
    norm()
    for c in range(POST_SPLIT):
        gate(c)
    finish()


def _post(ya, yb, ca, cb, h, p, layer, w_out, w_ple, w_gate, ple_norm, gate_norm, next_norm, tm, final):
    s, d = h.shape
    const = lambda shape, r=0: pl.BlockSpec(shape, lambda i: (r, 0), pipeline_mode=pl.Buffered(1))
    per_layer = lambda rows: pl.BlockSpec((None, rows, d), lambda i: (layer, 0, 0), pipeline_mode=pl.Buffered(1))
    row_block = pl.BlockSpec((tm, d), lambda i: (i, 0))
    out_f32, out_bf16 = jax.ShapeDtypeStruct((s, d), F32), jax.ShapeDtypeStruct((s, d), BF16)
    return pl.pallas_call(
        functools.partial(_post_kernel, final=final),
        grid=(s // tm,),
        in_specs=[pl.BlockSpec((tm, HALF), lambda i: (i, ca)),
                  pl.BlockSpec((tm, HALF), lambda i: (i, cb)),
                  row_block,
                  pl.BlockSpec((None, tm, PLE_DIM), lambda i: (layer, i, 0)),
                  const((HALF, d), 0), const((HALF, d), 1), per_layer(PLE_DIM), per_layer(d),
                  per_layer(1), per_layer(1), const((1, d))],
        out_specs=row_block if final else (row_block, row_block),
        out_shape=out_f32 if final else (out_f32, out_bf16),
        compiler_params=pltpu.CompilerParams(
            dimension_semantics=("arbitrary",), vmem_limit_bytes=VMEM_LIMIT),
        name="post_final" if final else "post",
    )(ya, yb, h, p, w_out, w_out, w_ple, w_gate, ple_norm, gate_norm, next_norm.reshape(1, d))


def _pack_odd_kernel(a_ref, b_ref, o_ref):
    j = pl.program_id(0)

    def emit(t):
        o_ref[...] = t.T.astype(BF16)

    @pl.when(j < _D_GATE)
    def _():
        emit(a_ref[...])

    @pl.when((j >= _D_GATE) & (j < _D_GLR))
    def _():
        emit(jnp.concatenate([a_ref[GATE_RANK_D:, :], b_ref[:GATE_RANK_D, :]], axis=0))

    @pl.when(j == _D_GLR)
    def _():
        emit(jnp.concatenate(
            [a_ref[:GATE_RANK_D, :], jnp.zeros((LANE - GATE_RANK_D, a_ref.shape[1]), F32)], axis=0))


def _pack_odd(w_t):
    n, d = w_t.shape
    n_blocks = PACKED_ODD // LANE
    src_a = lambda j: jnp.where(j == _D_GLR, _D_GATE, j)
    src_b = lambda j: jnp.clip(j + 1, _D_GATE + 1, pl.cdiv(n, LANE) - 1)
    return pl.pallas_call(
        _pack_odd_kernel,
        grid=(n_blocks,),
        in_specs=[pl.BlockSpec((LANE, d), lambda j: (src_a(j), 0)),
                  pl.BlockSpec((LANE, d), lambda j: (src_b(j), 0))],
        out_specs=pl.BlockSpec((d, LANE), lambda j: (0, j)),
        out_shape=jax.ShapeDtypeStruct((d, PACKED_ODD), BF16),
        compiler_params=pltpu.CompilerParams(
            dimension_semantics=("arbitrary",), vmem_limit_bytes=VMEM_LIMIT),
        name="pack_odd",
    )(w_t, w_t)


def kernel(x, p, norm_mix, w_in_even, sinks_a, vnorm_b, w_spatial_b, b_spatial_b, w_out_even,
           w_in_odd, lower_bounds_c, onorm_c, w_gate_up_d, b_gate_d, onorm_d, w_out_odd,
           w_ple_proj, ple_norm, ple_gate_norm, w_ple_gate, final_norm):
    assert x.shape[0] == 1 and x.shape[2] == D_MODEL and norm_mix.shape[0] == 2
    s = x.shape[1]
    h = x[0]
    p = p.reshape(2, s, PLE_DIM)
    w_ple, w_gate = w_ple_proj.astype(BF16), w_ple_gate.astype(BF16)
    ple_norm, gate_norm = ple_norm.reshape(2, 1, D_MODEL), ple_gate_norm.reshape(2, 1, D_MODEL)
    post = functools.partial(_post, p=p, w_ple=w_ple, w_gate=w_gate, ple_norm=ple_norm, gate_norm=gate_norm,
                             tm=256)

    z = _norm_matmul(h, norm_mix[0], w_in_even[0].astype(BF16), tm=min(s, 1024), tn=IN_EVEN // 3)
    h, hn = _layer0_tail(z, h, p, sinks_a[0], vnorm_b[0], w_spatial_b[0], b_spatial_b[0],
                         w_out_even[0].astype(BF16), w_ple, w_gate, ple_norm, gate_norm, norm_mix[1], tb=256)

    z = _norm_matmul(hn, None, _pack_odd(w_in_odd[0].T), tm=min(s, 1024), tn=PACKED_ODD // 3)
    sm = jax.nn.softmax(lower_bounds_c.astype(F32), axis=0)
    lb = (jnp.cumsum(sm, axis=0) - sm[0])[1]
    yc = _hgrn2(z, lb, onorm_c[0], tb=256, hp=8)
    yd = _gla(z, w_gate_up_d[0], b_gate_d[0], onorm_d[0], tb=256, hp=4)
    out = post(yc, yd, 0, 0, h, layer=1, w_out=w_out_odd[0].astype(BF16), next_norm=final_norm, final=True)
    return out[None]
```

```python
import functools
import math

import jax
import jax.numpy as jnp
from jax import lax
from jax.experimental import pallas as pl
from jax.experimental.pallas import tpu as pltpu

F32 = jnp.float32
BF16 = jnp.bfloat16

EPS = 1e-6
NEG_INF = -1e30

D_MODEL = 2048
PLE_DIM = 256
HALF = D_MODEL // 2
HEAD_A = 64
N_KV_A = 2
Q_PER_KV_A = 8
WINDOW = 128
N_GROUPS_B = 8
HEAD_K = 128
N_HEADS_C = 8
N_HEADS_D = 4
HEAD_V_C = 128
HEAD_V_D = 256
GATE_RANK_D = 16
GATE_LOGIT_NORM_D = 16.0
LA_CHUNK = 64
LA_SUB = 16
MAX_SAFE_STEP_DECAY = 60.0 / LA_SUB
LANE = 128
VMEM_LIMIT = 56 * 1024 * 1024


def _dot(a, b):
    return jnp.dot(a, b, preferred_element_type=F32)


def _dot_nt(a, b):
    return lax.dot_general(a, b, (((1,), (1,)), ((), ())), preferred_element_type=F32)


def _dot_tn(a, b):
    return lax.dot_general(a, b, (((0,), (0,)), ((), ())), preferred_element_type=F32)


def _rms(x, gain):
    return x * lax.rsqrt(jnp.mean(x * x, axis=-1, keepdims=True) + EPS) * gain


def _silu(x):
    return x * jax.nn.sigmoid(x)


def _gelu(x):
    return 0.5 * x * (1.0 + lax.erf(x * (2.0 ** -0.5)))


def _norm_matmul_kernel(x_ref, g_ref, w_ref, o_ref, xn_ref):
    @pl.when(pl.program_id(1) == 0)
    def _():
        xn_ref[...] = _rms(x_ref[...], g_ref[...]).astype(BF16)

    o_ref[...] = _dot(xn_ref[...], w_ref[...]).astype(o_ref.dtype)


def _matmul_kernel(x_ref, w_ref, o_ref):
    o_ref[...] = _dot_nt(x_ref[...], w_ref[...]).astype(o_ref.dtype)


def _norm_matmul(x, gain, w, tm, tn):
    s, d = x.shape
    n = w.shape[1] if gain is not None else w.shape[0]
    x_spec = pl.BlockSpec((tm, d), lambda i, j: (i, 0))
    w_spec = pl.BlockSpec((d, tn), lambda i, j: (0, j))
    common = dict(
        grid=(s // tm, n // tn),
        out_specs=pl.BlockSpec((tm, tn), lambda i, j: (i, j)),
        out_shape=jax.ShapeDtypeStruct((s, n), BF16),
        compiler_params=pltpu.CompilerParams(
            dimension_semantics=("arbitrary", "arbitrary"), vmem_limit_bytes=VMEM_LIMIT))
    if gain is None:
        w_spec = pl.BlockSpec((tn, d), lambda i, j: (j, 0))
        return pl.pallas_call(_matmul_kernel, in_specs=[x_spec, w_spec], name="matmul", **common)(x, w)
    return pl.pallas_call(
        _norm_matmul_kernel,
        in_specs=[x_spec, pl.BlockSpec((1, d), lambda i, j: (0, 0)), w_spec],
        scratch_shapes=[pltpu.VMEM((tm, d), BF16)],
        name="norm_matmul", **common)(x, gain.reshape(1, d), w)


_A_Q, _A_K, _A_V, _A_GATE, _B_U, _B_V, _B_GATE = 0, 1024, 1152, 1280, 2304, 3328, 4352
IN_EVEN = 5376


def _mixer0_phases(sinks_ref, z_ref, kp_ref, vp_ref, vn_ref, wsp_ref, bsp_ref, y_ref, first, r):
    lane = lax.broadcasted_iota(jnp.int32, (1, LANE), 1)
    lo = lane < HEAD_A
    qi = lax.broadcasted_iota(jnp.int32, (WINDOW, 2 * WINDOW), 0)
    kj = lax.broadcasted_iota(jnp.int32, (WINDOW, 2 * WINDOW), 1)
    band = (kj > qi) & (kj <= qi + WINDOW)
    tri = (lax.broadcasted_iota(jnp.int32, (WINDOW, WINDOW), 0)
           >= lax.broadcasted_iota(jnp.int32, (WINDOW, WINDOW), 1))
    zero_bf = jnp.zeros((), BF16)
    n_q = Q_PER_KV_A
    rows = pl.ds(r * WINDOW, WINDOW)
    zcols = lambda c0, w: z_ref[rows, pl.ds(c0, w)]
    st = {}

    def swap_halves(t):
        return jnp.concatenate([t[:, HEAD_A:], t[:, :HEAD_A]], axis=1)

    def both_halves(t, t_sw, h):
        return jnp.where(lo, t, t_sw) if h == 0 else jnp.where(lo, t_sw, t)

    def scores():
        if r == 0:
            k_prev, v_prev = kp_ref[...], vp_ref[...]
            st["mask"] = band & ((kj >= WINDOW) | jnp.logical_not(first))
        else:
            prev = pl.ds((r - 1) * WINDOW, WINDOW)
            k_prev, v_prev = z_ref[prev, pl.ds(_A_K, LANE)], z_ref[prev, pl.ds(_A_V, LANE)]
            st["mask"] = band
        kk = jnp.concatenate([k_prev, zcols(_A_K, LANE)], axis=0)
        vv = jnp.concatenate([v_prev, zcols(_A_V, LANE)], axis=0)
        kk_sw = swap_halves(kk)
        st["vv"], st["vv_sw"] = vv, swap_halves(vv)
        st["scores"] = []
        for h in range(N_KV_A):
            parts = []
            for j in range(n_q // 2):
                qp = zcols(_A_Q + (h * n_q + 2 * j) * HEAD_A, LANE) * jnp.asarray(HEAD_A ** -0.5, BF16)
                parts += [jnp.where(lo, qp, zero_bf), jnp.where(lo, zero_bf, qp)]
            st["scores"].append(_dot_nt(jnp.concatenate(parts, axis=0), both_halves(kk, kk_sw, h)))

    def gmlp():
        u = _gelu(zcols(_B_U, HALF).astype(F32))
        vg = _rms(_gelu(zcols(_B_V, HALF).astype(F32)), vn_ref[...]).astype(BF16)
        gb = zcols(_B_GATE, HALF).astype(F32)
        for g in range(N_GROUPS_B):
            cols = slice(g * LANE, (g + 1) * LANE)
            w = jnp.where(tri, wsp_ref[g], 0.0).astype(BF16)
            mixed = _dot(w, vg[:, cols]) + bsp_ref[:, g:g + 1]
            y_ref[rows, pl.ds(HALF + g * LANE, LANE)] = (u[:, cols] * mixed * _silu(gb[:, cols])).astype(BF16)

    def softmax_pv():
        mask, vv, vv_sw = st["mask"], st["vv"], st["vv_sw"]
        for h in range(N_KV_A):
            ps, rs = [], []
            for i in range(n_q):
                sink = sinks_ref[h * n_q + i]
                sc = jnp.where(mask, st["scores"][h][i * WINDOW:(i + 1) * WINDOW], NEG_INF)
                m = jnp.maximum(jnp.max(sc, axis=-1, keepdims=True), sink)
                p = jnp.exp(sc - m)
                rs.append(1.0 / (jnp.sum(p, axis=-1, keepdims=True) + jnp.exp(sink - m)))
                ps.append(p.astype(BF16))
            v_both = both_halves(vv, vv_sw, h)
            v2 = jnp.concatenate([jnp.where(lo, v_both, zero_bf), jnp.where(lo, zero_bf, v_both)], axis=0)
            p_pairs = jnp.concatenate(
                [jnp.concatenate(ps[2 * j:2 * j + 2], axis=1) for j in range(n_q // 2)], axis=0)
            o = _dot(p_pairs, v2)
            for j in range(n_q // 2):
                c0 = (h * n_q + 2 * j) * HEAD_A
                oj = o[j * WINDOW:(j + 1) * WINDOW] * jnp.where(lo, rs[2 * j], rs[2 * j + 1])
                g = zcols(_A_GATE + c0, LANE).astype(F32)
                y_ref[rows, pl.ds(c0, LANE)] = (oj * _silu(g)).astype(BF16)

    return scores, gmlp, softmax_pv


POST_SPLIT = 4


def _post_phases(ya, yb, h_ref, p_ref, woa_ref, wob_ref, wple_ref, wg_ref, pn_ref, gn_ref, nn_ref, out_refs, final):
    cw = D_MODEL // POST_SPLIT
    st = {"h1": [], "gate": []}

    def out_proj(c):
        cols = pl.ds(c * cw, cw)
        st["h1"].append(h_ref[:, cols] + _dot(ya(), woa_ref[:, cols]) + _dot(yb(), wob_ref[:, cols]))

    def embed():
        st["e"] = _rms(_dot(p_ref[...].astype(BF16), wple_ref[...]), pn_ref[...])

    def norm():
        st["h1"] = jnp.concatenate(st["h1"], axis=1)
        st["hn1"] = _rms(st["h1"], gn_ref[...]).astype(BF16)

    def gate(c):
        st["gate"].append(jax.nn.sigmoid(_dot(st["hn1"], wg_ref[:, pl.ds(c * cw, cw)])))

    def finish():
        h2 = st["h1"] + st["e"] * jnp.concatenate(st["gate"], axis=1)
        normed = _rms(h2, nn_ref[...])
        if final:
            out_refs[0][...] = normed
        else:
            out_refs[0][...] = h2
            out_refs[1][...] = normed.astype(BF16)

    return out_proj, embed, norm, gate, finish


def _layer0_tail_kernel(sinks_ref, z_ref, kp_ref, vp_ref, vn_ref, wsp_ref, bsp_ref,
                        h_ref, p_ref, woa_ref, wob_ref, wple_ref, wg_ref, pn_ref, gn_ref, nn_ref,
                        h_out_ref, hn_out_ref, y_buf, *, tb):
    i = pl.program_id(0)

    @pl.when(i == 0)
    def _():
        y_buf[...] = jnp.zeros_like(y_buf)

    slot = lax.rem(i, 2)
    y_new, y_old = y_buf.at[slot], y_buf.at[1 - slot]
    out_proj, embed, norm, gate, finish = _post_phases(
        lambda: y_old[:, :HALF], lambda: y_old[:, HALF:], h_ref, p_ref, woa_ref, wob_ref, wple_ref, wg_ref,
        pn_ref, gn_ref, nn_ref, (h_out_ref, hn_out_ref), final=False)
    mix = [_mixer0_phases(sinks_ref, z_ref, kp_ref, vp_ref, vn_ref, wsp_ref, bsp_ref, y_new, i == 0, r)
           for r in range(tb // WINDOW)]
    assert len(mix) == 2 and POST_SPLIT == 4
    (scores0, gmlp0, pv0), (scores1, gmlp1, pv1) = mix
    for piece in (lambda: out_proj(0), lambda: out_proj(1), scores0, lambda: out_proj(2), lambda: out_proj(3),
                  gmlp0, embed, pv0, norm, lambda: gate(0), scores1, lambda: gate(1), gmlp1,
                  lambda: gate(2), pv1, lambda: gate(3), finish):
        piece()


def _layer0_tail(z, h, p, sinks, vnorm, w_sp, b_sp, w_out, w_ple, w_gate, ple_norm, gate_norm, next_norm, tb):
    s, d = h.shape
    n, nb = s // tb, tb // WINDOW
    cur = lambda i: jnp.minimum(i, n - 1)
    old = lambda i: jnp.maximum(i - 1, 0)
    prv = lambda c0: pl.BlockSpec((WINDOW, LANE), lambda i: (jnp.maximum(cur(i) * nb - 1, 0), c0 // LANE))
    full = lambda a: pl.BlockSpec(a.shape, lambda i: (0,) * a.ndim)
    const = lambda shape, r=0: pl.BlockSpec(shape, lambda i: (r, 0), pipeline_mode=pl.Buffered(1))
    per_layer = lambda rows: pl.BlockSpec((None, rows, d), lambda i: (0, 0, 0), pipeline_mode=pl.Buffered(1))
    row_block = pl.BlockSpec((tb, d), lambda i: (old(i), 0))
    vnorm = vnorm.reshape(1, HALF)
    b_t = b_sp.T
    return pl.pallas_call(
        functools.partial(_layer0_tail_kernel, tb=tb),
        grid=(n + 1,),
        in_specs=[pl.BlockSpec(memory_space=pltpu.SMEM),
                  pl.BlockSpec((tb, IN_EVEN), lambda i: (cur(i), 0)), prv(_A_K), prv(_A_V),
                  full(vnorm), full(w_sp), full(b_t),
                  row_block, pl.BlockSpec((None, tb, PLE_DIM), lambda i: (0, old(i), 0)),
                  const((HALF, d), 0), const((HALF, d), 1), per_layer(PLE_DIM), per_layer(d),
                  per_layer(1), per_layer(1), const((1, d))],
        out_specs=(row_block, row_block),
        out_shape=(jax.ShapeDtypeStruct((s, d), F32), jax.ShapeDtypeStruct((s, d), BF16)),
        scratch_shapes=[pltpu.VMEM((2, tb, d), BF16)],
        compiler_params=pltpu.CompilerParams(
            dimension_semantics=("arbitrary",), vmem_limit_bytes=VMEM_LIMIT),
        name="layer0_tail",
    )(sinks, z, z, z, vnorm, w_sp, b_t, h, p, w_out, w_out, w_ple, w_gate, ple_norm, gate_norm,
      next_norm.reshape(1, d))


def _chunk_cumsum(log_g, tb):
    t = lax.broadcasted_iota(jnp.int32, (tb, tb), 0)
    s = lax.broadcasted_iota(jnp.int32, (tb, tb), 1)
    tri = jnp.where((s <= t) & (t // LA_CHUNK == s // LA_CHUNK), 1.0, 0.0).astype(BF16)
    hi = log_g.astype(BF16)
    low = (log_g - hi.astype(F32)).astype(BF16)
    return _dot(tri, hi) + _dot(tri, low)


def _att_exact(qc, g, kbuf, gbuf, r0):
    row = lax.broadcasted_iota(jnp.int32, (LA_CHUNK, LA_CHUNK), 0)
    col = lax.broadcasted_iota(jnp.int32, (LA_CHUNK, LA_CHUNK), 1)
    sub_pos = lax.broadcasted_iota(jnp.int32, (LA_CHUNK, 1), 0) % LA_SUB
    zeros = lambda n: jnp.zeros((n, HEAD_K), F32)
    kc = kbuf[pl.ds(LA_SUB + r0, LA_CHUNK), :]
    att = jnp.zeros((LA_CHUNK, LA_CHUNK), F32)
    for d in range(LA_SUB):
        ks = kbuf[pl.ds(LA_SUB + r0 - d, LA_CHUNK), :]
        gs = gbuf[pl.ds(LA_SUB + r0 - d, LA_CHUNK), :]
        pd = jnp.where(sub_pos >= d, qc * ks * jnp.exp2(g - gs), 0.0)
        att = jnp.where(col == row - d, jnp.sum(pd, axis=-1, keepdims=True), att)
    m = LA_SUB
    while m < LA_CHUNK:
        q_parts, k_parts = [], []
        for b0 in range(0, LA_CHUNK, 2 * m):
            ref_row = g[b0 + m - 1:b0 + m]
            lower, upper = slice(b0, b0 + m), slice(b0 + m, b0 + 2 * m)
            q_parts += [zeros(m), qc[upper] * jnp.exp2(g[upper] - ref_row)]
            k_parts += [kc[lower] * jnp.exp2(ref_row - g[lower]), zeros(m)]
        part = _dot_nt(jnp.concatenate(q_parts, axis=0).astype(BF16),
                       jnp.concatenate(k_parts, axis=0).astype(BF16))
        if 2 * m < LA_CHUNK:
            part = jnp.where(row // (2 * m) == col // (2 * m), part, 0.0)
        att = att + part
        m *= 2
    return att


def _att_factored(qc, kc, g, q_in):
    row = lax.broadcasted_iota(jnp.int32, (LA_CHUNK, LA_CHUNK), 0)
    col = lax.broadcasted_iota(jnp.int32, (LA_CHUNK, LA_CHUNK), 1)
    zeros = lambda n: jnp.zeros((n, HEAD_K), BF16)
    q_slabs, k_slabs = [], []
    for b0 in range(0, LA_CHUNK, LA_SUB):
        blk = slice(b0, b0 + LA_SUB)
        ref_row = g[b0 - 1:b0] if b0 else 0.0
        q_part = (qc[b0:] * jnp.exp2(g[b0:] - ref_row)).astype(BF16) if b0 else q_in
        k_part = (kc[blk] * jnp.exp2(ref_row - g[blk])).astype(BF16)
        q_slabs.append(jnp.concatenate([zeros(b0), q_part], axis=0) if b0 else q_part)
        k_slabs.append(jnp.concatenate(
            [z for z in (zeros(b0), k_part, zeros(LA_CHUNK - LA_SUB - b0)) if z.shape[0]], axis=0))
    att = _dot_nt(jnp.concatenate(q_slabs, axis=1), jnp.concatenate(k_slabs, axis=1))
    return jnp.where(col <= row, att, 0.0)


def _linear_attention_heads(qs, ks, vs, log_g_fns, safe, st_ref, kbuf, gbuf, emit, tb):
    n_heads, n_chunks = len(qs), tb // LA_CHUNK
    chunk = lambda a, c: a[c * LA_CHUNK:(c + 1) * LA_CHUNK]

    def run(exact):
        gcs = [_chunk_cumsum(fn(), tb) for fn in log_g_fns]
        q_in = {(j, c): (chunk(qs[j], c) * jnp.exp2(chunk(gcs[j], c))).astype(BF16)
                for j in range(n_heads) for c in range(n_chunks)}
        att = {}
        for j in range(n_heads):
            if exact:
                kbuf[j, pl.ds(0, LA_SUB), :] = jnp.zeros((LA_SUB, HEAD_K), F32)
                gbuf[j, pl.ds(0, LA_SUB), :] = jnp.zeros((LA_SUB, HEAD_K), F32)
                kbuf[j, pl.ds(LA_SUB, tb), :] = ks[j]
                gbuf[j, pl.ds(LA_SUB, tb), :] = gcs[j]
            for c in range(n_chunks):
                if exact:
                    att[j, c] = _att_exact(chunk(qs[j], c), chunk(gcs[j], c), kbuf.at[j], gbuf.at[j], c * LA_CHUNK)
                else:
                    att[j, c] = _att_factored(chunk(qs[j], c), chunk(ks[j], c), chunk(gcs[j], c), q_in[j, c])
        intra, updates, decays = {}, {}, {}
        for j in range(n_heads):
            for c in range(n_chunks):
                kc, vc, g = chunk(ks[j], c), chunk(vs[j], c), chunk(gcs[j], c)
                g_last = g[LA_CHUNK - 1:LA_CHUNK]
                intra[j, c] = _dot(att[j, c].astype(BF16), vc)
                updates[j, c] = _dot_tn(vc, (kc * jnp.exp2(g_last - g)).astype(BF16))
                decays[j, c] = jnp.exp2(g_last)
        states = {}
        for j in range(n_heads):
            st = st_ref[j]
            for c in range(n_chunks):
                states[j, c] = st.astype(BF16)
                st = st * decays[j, c] + updates[j, c]
            st_ref[j] = st
        for j in range(n_heads):
            emit(j, jnp.concatenate(
                [_dot_nt(q_in[j, c], states[j, c]) + intra[j, c]
                 for c in range(n_chunks)], axis=0))

    pl.when(safe)(lambda: run(exact=False))
    pl.when(jnp.logical_not(safe))(lambda: run(exact=True))


def _la_scratch(tb, dv, hp):
    return [pltpu.VMEM((hp, dv, HEAD_K), F32),
            pltpu.VMEM((hp, LA_SUB + tb, HEAD_K), F32),
            pltpu.VMEM((hp, LA_SUB + tb, HEAD_K), F32)]


def _head_cols(j, width):
    return slice(j * width, (j + 1) * width)


def _gated_head_norm(y_ref, on_ref, gate_ref, width):
    def emit(j, o):
        cols = _head_cols(j, width)
        y_ref[:, cols] = (_rms(o, on_ref[:, cols]) * _silu(gate_ref[:, cols].astype(F32))).astype(BF16)
    return emit


def _hgrn2_kernel(q_ref, f_ref, i_ref, gate_ref, lb_ref, on_ref, y_ref, st_ref, kbuf, gbuf, *, tb, hp):
    @pl.when(pl.program_id(1) == 0)
    def _():
        st_ref[...] = jnp.zeros_like(st_ref)

    qs, ks, vs, fs = [], [], [], []
    for j in range(hp):
        cols = _head_cols(j, HEAD_K)
        lb = lb_ref[:, cols]
        f = lb + (1.0 - lb) * jax.nn.sigmoid(f_ref[:, cols].astype(F32))
        qs.append(q_ref[:, cols].astype(F32) * (HEAD_K ** -0.5))
        ks.append(1.0 - f)
        vs.append(i_ref[:, cols])
        fs.append(f)
    safe = jnp.min(functools.reduce(jnp.minimum, fs)) >= math.exp(-MAX_SAFE_STEP_DECAY)
    _linear_attention_heads(qs, ks, vs, [functools.partial(jnp.log2, f) for f in fs], safe, st_ref, kbuf, gbuf,
                            _gated_head_norm(y_ref, on_ref, gate_ref, HEAD_V_C), tb)


def _gla_kernel(q_ref, k_ref, v_ref, gate_ref, glr_ref, wup_ref, bg_ref, on_ref, y_ref,
                st_ref, kbuf, gbuf, *, tb, hp):
    @pl.when(pl.program_id(1) == 0)
    def _():
        st_ref[...] = jnp.zeros_like(st_ref)

    x = _dot(glr_ref[...], wup_ref[...]) + bg_ref[...]
    log_sigmoid = lambda t: -(jnp.maximum(-t, 0.0) + jnp.log1p(jnp.exp(-jnp.abs(t))))
    to_log2 = math.log2(math.e) / GATE_LOGIT_NORM_D
    qs, ks, vs, lgs = [], [], [], []
    for j in range(hp):
        cols = _head_cols(j, HEAD_K)
        qs.append(q_ref[:, cols].astype(F32) * (HEAD_K ** -0.5))
        ks.append(k_ref[:, cols].astype(F32))
        vs.append(v_ref[:, _head_cols(j, HEAD_V_D)])
        lgs.append(lambda cols=cols: log_sigmoid(x[:, cols]) * to_log2)
    safe = jnp.max(-x) <= MAX_SAFE_STEP_DECAY * GATE_LOGIT_NORM_D - math.log(2.0)
    _linear_attention_heads(qs, ks, vs, lgs, safe, st_ref, kbuf, gbuf,
                            _gated_head_norm(y_ref, on_ref, gate_ref, HEAD_V_D), tb)


_C_Q, _C_F, _C_I, _C_GATE = 0, 8, 16, 24
_D_Q, _D_K, _D_V, _D_GATE, _D_GLR = 32, 36, 40, 48, 56
PACKED_ODD = 57 * LANE


def _hgrn2(z, lb, onorm, tb, hp):
    s = z.shape[0]
    kw, vw = hp * HEAD_K, hp * HEAD_V_C
    col = lambda c0: pl.BlockSpec((tb, kw), lambda h, i: (i, c0 // hp + h))
    vec = lambda w: pl.BlockSpec((1, w), lambda h, i: (0, h))
    return pl.pallas_call(
        functools.partial(_hgrn2_kernel, tb=tb, hp=hp),
        grid=(N_HEADS_C // hp, s // tb),
        in_specs=[col(_C_Q), col(_C_F), col(_C_I), col(_C_GATE), vec(kw), vec(vw)],
        out_specs=pl.BlockSpec((tb, vw), lambda h, i: (i, h)),
        out_shape=jax.ShapeDtypeStruct((s, HALF), BF16),
        scratch_shapes=_la_scratch(tb, HEAD_V_C, hp),
        compiler_params=pltpu.CompilerParams(
            dimension_semantics=("arbitrary", "arbitrary"), vmem_limit_bytes=VMEM_LIMIT),
        name="hgrn2",
    )(z, z, z, z, lb.reshape(1, HALF), onorm.reshape(1, HALF))


def _gla(z, w_up, b_gate, onorm, tb, hp):
    s = z.shape[0]
    kw, vw = hp * HEAD_K, hp * HEAD_V_D
    col = lambda c0, w: pl.BlockSpec((tb, w), lambda h, i: (i, c0 * LANE // w + h))
    vec = lambda w: pl.BlockSpec((1, w), lambda h, i: (0, h))
    w_up = jnp.pad(w_up, ((0, LANE - GATE_RANK_D), (0, 0))).astype(BF16)
    return pl.pallas_call(
        functools.partial(_gla_kernel, tb=tb, hp=hp),
        grid=(N_HEADS_D // hp, s // tb),
        in_specs=[col(_D_Q, kw), col(_D_K, kw), col(_D_V, vw), col(_D_GATE, vw),
                  pl.BlockSpec((tb, LANE), lambda h, i: (i, _D_GLR)),
                  pl.BlockSpec((LANE, kw), lambda h, i: (0, h)),
                  vec(kw), vec(vw)],
        out_specs=pl.BlockSpec((tb, vw), lambda h, i: (i, h)),
        out_shape=jax.ShapeDtypeStruct((s, HALF), BF16),
        scratch_shapes=_la_scratch(tb, HEAD_V_D, hp),
        compiler_params=pltpu.CompilerParams(
            dimension_semantics=("arbitrary", "arbitrary"), vmem_limit_bytes=VMEM_LIMIT),
        name="gla",
    )(z, z, z, z, z, w_up, b_gate.reshape(1, N_HEADS_D * HEAD_K), onorm.reshape(1, HALF))


def _post_kernel(ya_ref, yb_ref, h_ref, p_ref, woa_ref, wob_ref, wple_ref, wg_ref,
                 pn_ref, gn_ref, nn_ref, *out_refs, final):
    out_proj, embed, norm, gate, finish = _post_phases(
        lambda: ya_ref[...], lambda: yb_ref[...], h_ref, p_ref, woa_ref, wob_ref, wple_ref, wg_ref,
        pn_ref, gn_ref, nn_ref, out_refs, final)
    for c in range(POST_SPLIT):
        out_proj(c)
    embed()
    norm()
    for c in range(POST_SPLIT):
        gate(c)
    finish()


def _post(ya, yb, ca, cb, h, p, layer, w_out, w_ple, w_gate, ple_norm, gate_norm, next_norm, tm, final):
    s, d = h.shape
    const = lambda shape, r=0: pl.BlockSpec(shape, lambda i: (r, 0), pipeline_mode=pl.Buffered(1))
    per_layer = lambda rows: pl.BlockSpec((None, rows, d), lambda i: (layer, 0, 0), pipeline_mode=pl.Buffered(1))
    row_block = pl.BlockSpec((tm, d), lambda i: (i, 0))
    out_f32, out_bf16 = jax.ShapeDtypeStruct((s, d), F32), jax.ShapeDtypeStruct((s, d), BF16)
    return pl.pallas_call(
        functools.partial(_post_kernel, final=final),
        grid=(s // tm,),
        in_specs=[pl.BlockSpec((tm, HALF), lambda i: (i, ca)),
                  pl.BlockSpec((tm, HALF), lambda i: (i, cb)),
                  row_block,
                  pl.BlockSpec((None, tm, PLE_DIM), lambda i: (layer, i, 0)),
                  const((HALF, d), 0), const((HALF, d), 1), per_layer(PLE_DIM), per_layer(d),
                  per_layer(1), per_layer(1), const((1, d))],
        out_specs=row_block if final else (row_block, row_block),
        out_shape=out_f32 if final else (out_f32, out_bf16),
        compiler_params=pltpu.CompilerParams(
            dimension_semantics=("arbitrary",), vmem_limit_bytes=VMEM_LIMIT),
        name="post_final" if final else "post",
    )(ya, yb, h, p, w_out, w_out, w_ple, w_gate, ple_norm, gate_norm, next_norm.reshape(1, d))


def _pack_odd_kernel(a_ref, b_ref, o_ref):
    j = pl.program_id(0)

    def emit(t):
        o_ref[...] = t.astype(BF16)

    @pl.when(j < _D_GATE)
    def _():
        emit(a_ref[...])

    @pl.when((j >= _D_GATE) & (j < _D_GLR))
    def _():
        emit(jnp.concatenate([a_ref[GATE_RANK_D:, :], b_ref[:GATE_RANK_D, :]], axis=0))

    @pl.when(j == _D_GLR)
    def _():
        emit(jnp.concatenate(
            [a_ref[:GATE_RANK_D, :], jnp.zeros((LANE - GATE_RANK_D, a_ref.shape[1]), F32)], axis=0))


def _pack_odd(w_t):
    n, d = w_t.shape
    n_blocks = PACKED_ODD // LANE
    src_a = lambda j: jnp.where(j == _D_GLR, _D_GATE, j)
    src_b = lambda j: jnp.clip(j + 1, _D_GATE + 1, pl.cdiv(n, LANE) - 1)
    return pl.pallas_call(
        _pack_odd_kernel,
        grid=(n_blocks,),
        in_specs=[pl.BlockSpec((LANE, d), lambda j: (src_a(j), 0)),
                  pl.BlockSpec((LANE, d), lambda j: (src_b(j), 0))],
        out_specs=pl.BlockSpec((LANE, d), lambda j: (j, 0)),
        out_shape=jax.ShapeDtypeStruct((PACKED_ODD, d), BF16),
        compiler_params=pltpu.CompilerParams(
            dimension_semantics=("arbitrary",), vmem_limit_bytes=VMEM_LIMIT),
        name="pack_odd",
    )(w_t, w_t)


def kernel(x, p, norm_mix, w_in_even, sinks_a, vnorm_b, w_spatial_b, b_spatial_b, w_out_even,
           w_in_odd, lower_bounds_c, onorm_c, w_gate_up_d, b_gate_d, onorm_d, w_out_odd,
           w_ple_proj, ple_norm, ple_gate_norm, w_ple_gate, final_norm):
    assert x.shape[0] == 1 and x.shape[2] == D_MODEL and norm_mix.shape[0] == 2
    s = x.shape[1]
    h = x[0]
    p = p.reshape(2, s, PLE_DIM)
    w_ple, w_gate = w_ple_proj.astype(BF16), w_ple_gate.astype(BF16)
    ple_norm, gate_norm = ple_norm.reshape(2, 1, D_MODEL), ple_gate_norm.reshape(2, 1, D_MODEL)
    post = functools.partial(_post, p=p, w_ple=w_ple, w_gate=w_gate, ple_norm=ple_norm, gate_norm=gate_norm,
                             tm=256)

    z = _norm_matmul(h, norm_mix[0], w_in_even[0].astype(BF16), tm=min(s, 1024), tn=IN_EVEN // 3)
    h, hn = _layer0_tail(z, h, p, sinks_a[0], vnorm_b[0], w_spatial_b[0], b_spatial_b[0],
                         w_out_even[0].astype(BF16), w_ple, w_gate, ple_norm, gate_norm, norm_mix[1], tb=256)

    z = _norm_matmul(hn, None, _pack_odd(w_in_odd[0].T), tm=min(s, 1024), tn=PACKED_ODD // 3)
    sm = jax.nn.softmax(lower_bounds_c.astype(F32), axis=0)
    lb = (jnp.cumsum(sm, axis=0) - sm[0])[1]
    yc = _hgrn2(z, lb, onorm_c[0], tb=256, hp=8)
    yd = _gla(z, w_gate_up_d[0], b_gate_d[0], onorm_d[0], tb=256, hp=4)
    out = post(yc, yd, 0, 0, h, layer=1, w_out=w_out_odd[0].astype(BF16), next_norm=final_norm, final=True)
    return out[None]
```

```python
import functools
import math

import jax
import jax.numpy as jnp
from jax import lax
from jax.experimental import pallas as pl
from jax.experimental.pallas import tpu as pltpu

F32 = jnp.float32
BF16 = jnp.bfloat16

EPS = 1e-6
NEG_INF = -1e30

D_MODEL = 2048
PLE_DIM = 256
HALF = D_MODEL // 2
HEAD_A = 64
N_KV_A = 2
Q_PER_KV_A = 8
WINDOW = 128
N_GROUPS_B = 8
HEAD_K = 128
N_HEADS_C = 8
N_HEADS_D = 4
HEAD_V_C = 128
HEAD_V_D = 256
GATE_RANK_D = 16
GATE_LOGIT_NORM_D = 16.0
LA_CHUNK = 64
LA_SUB = 16
MAX_SAFE_STEP_DECAY = 60.0 / LA_SUB
LANE = 128
VMEM_LIMIT = 56 * 1024 * 1024


def _dot(a, b):
    return jnp.dot(a, b, preferred_element_type=F32)


def _dot_nt(a, b):
    return lax.dot_general(a, b, (((1,), (1,)), ((), ())), preferred_element_type=F32)


def _dot_tn(a, b):
    return lax.dot_general(a, b, (((0,), (0,)), ((), ())), preferred_element_type=F32)


def _rms(x, gain):
    return x * lax.rsqrt(jnp.mean(x * x, axis=-1, keepdims=True) + EPS) * gain


def _silu(x):
    return x * jax.nn.sigmoid(x)


def _gelu(x):
    return 0.5 * x * (1.0 + lax.erf(x * (2.0 ** -0.5)))


def _norm_matmul_kernel(x_ref, g_ref, w_ref, o_ref, xn_ref):
    @pl.when(pl.program_id(1) == 0)
    def _():
        xn_ref[...] = _rms(x_ref[...], g_ref[...]).astype(BF16)

    o_ref[...] = _dot(xn_ref[...], w_ref[...]).astype(o_ref.dtype)


def _matmul_kernel(x_ref, w_ref, o_ref):
    o_ref[...] = _dot_nt(x_ref[...], w_ref[...]).astype(o_ref.dtype)


def _norm_matmul(x, gain, w, tm, tn):
    s, d = x.shape
    n = w.shape[1] if gain is not None else w.shape[0]
    x_spec = pl.BlockSpec((tm, d), lambda i, j: (i, 0))
    w_spec = pl.BlockSpec((d, tn), lambda i, j: (0, j))
    common = dict(
        grid=(s // tm, n // tn),
        out_specs=pl.BlockSpec((tm, tn), lambda i, j: (i, j)),
        out_shape=jax.ShapeDtypeStruct((s, n), BF16),
        compiler_params=pltpu.CompilerParams(
            dimension_semantics=("arbitrary", "arbitrary"), vmem_limit_bytes=VMEM_LIMIT))
    if gain is None:
        w_spec = pl.BlockSpec((tn, d), lambda i, j: (j, 0))
        return pl.pallas_call(_matmul_kernel, in_specs=[x_spec, w_spec], name="matmul", **common)(x, w)
    return pl.pallas_call(
        _norm_matmul_kernel,
        in_specs=[x_spec, pl.BlockSpec((1, d), lambda i, j: (0, 0)), w_spec],
        scratch_shapes=[pltpu.VMEM((tm, d), BF16)],
        name="norm_matmul", **common)(x, gain.reshape(1, d), w)


_A_Q, _A_K, _A_V, _A_GATE, _B_U, _B_V, _B_GATE = 0, 1024, 1152, 1280, 2304, 3328, 4352
IN_EVEN = 5376


def _mixer0_phases(sinks_ref, z_ref, kp_ref, vp_ref, vn_ref, wsp_ref, bsp_ref, y_ref, first, r):
    lane = lax.broadcasted_iota(jnp.int32, (1, LANE), 1)
    lo = lane < HEAD_A
    qi = lax.broadcasted_iota(jnp.int32, (WINDOW, 2 * WINDOW), 0)
    kj = lax.broadcasted_iota(jnp.int32, (WINDOW, 2 * WINDOW), 1)
    band = (kj > qi) & (kj <= qi + WINDOW)
    tri = (lax.broadcasted_iota(jnp.int32, (WINDOW, WINDOW), 0)
           >= lax.broadcasted_iota(jnp.int32, (WINDOW, WINDOW), 1))
    zero_bf = jnp.zeros((), BF16)
    n_q = Q_PER_KV_A
    rows = pl.ds(r * WINDOW, WINDOW)
    zcols = lambda c0, w: z_ref[rows, pl.ds(c0, w)]
    st = {}

    def swap_halves(t):
        return jnp.concatenate([t[:, HEAD_A:], t[:, :HEAD_A]], axis=1)

    def both_halves(t, t_sw, h):
        return jnp.where(lo, t, t_sw) if h == 0 else jnp.where(lo, t_sw, t)

    def scores():
        if r == 0:
            k_prev, v_prev = kp_ref[...], vp_ref[...]
            st["mask"] = band & ((kj >= WINDOW) | jnp.logical_not(first))
        else:
            prev = pl.ds((r - 1) * WINDOW, WINDOW)
            k_prev, v_prev = z_ref[prev, pl.ds(_A_K, LANE)], z_ref[prev, pl.ds(_A_V, LANE)]
            st["mask"] = band
        kk = jnp.concatenate([k_prev, zcols(_A_K, LANE)], axis=0)
        vv = jnp.concatenate([v_prev, zcols(_A_V, LANE)], axis=0)
        kk_sw = swap_halves(kk)
        st["vv"], st["vv_sw"] = vv, swap_halves(vv)
        st["scores"] = []
        for h in range(N_KV_A):
            parts = []
            for j in range(n_q // 2):
                qp = zcols(_A_Q + (h * n_q + 2 * j) * HEAD_A, LANE) * jnp.asarray(HEAD_A ** -0.5, BF16)
                parts += [jnp.where(lo, qp, zero_bf), jnp.where(lo, zero_bf, qp)]
            st["scores"].append(_dot_nt(jnp.concatenate(parts, axis=0), both_halves(kk, kk_sw, h)))

    def gmlp():
        u = _gelu(zcols(_B_U, HALF).astype(F32))
        vg = _rms(_gelu(zcols(_B_V, HALF).astype(F32)), vn_ref[...]).astype(BF16)
        gb = zcols(_B_GATE, HALF).astype(F32)
        for g in range(N_GROUPS_B):
            cols = slice(g * LANE, (g + 1) * LANE)
            w = jnp.where(tri, wsp_ref[g], 0.0).astype(BF16)
            mixed = _dot(w, vg[:, cols]) + bsp_ref[:, g:g + 1]
            y_ref[rows, pl.ds(HALF + g * LANE, LANE)] = (u[:, cols] * mixed * _silu(gb[:, cols])).astype(BF16)

    def softmax_pv():
        mask, vv, vv_sw = st["mask"], st["vv"], st["vv_sw"]
        for h in range(N_KV_A):
            ps, rs = [], []
            for i in range(n_q):
                sink = sinks_ref[h * n_q + i]
                sc = jnp.where(mask, st["scores"][h][i * WINDOW:(i + 1) * WINDOW], NEG_INF)
                m = jnp.maximum(jnp.max(sc, axis=-1, keepdims=True), sink)
                p = jnp.exp(sc - m)
                rs.append(1.0 / (jnp.sum(p, axis=-1, keepdims=True) + jnp.exp(sink - m)))
                ps.append(p.astype(BF16))
            v_both = both_halves(vv, vv_sw, h)
            v2 = jnp.concatenate([jnp.where(lo, v_both, zero_bf), jnp.where(lo, zero_bf, v_both)], axis=0)
            p_pairs = jnp.concatenate(
                [jnp.concatenate(ps[2 * j:2 * j + 2], axis=1) for j in range(n_q // 2)], axis=0)
            o = _dot(p_pairs, v2)
            for j in range(n_q // 2):
                c0 = (h * n_q + 2 * j) * HEAD_A
                oj = o[j * WINDOW:(j + 1) * WINDOW] * jnp.where(lo, rs[2 * j], rs[2 * j + 1])
                g = zcols(_A_GATE + c0, LANE).astype(F32)
                y_ref[rows, pl.ds(c0, LANE)] = (oj * _silu(g)).astype(BF16)

    return scores, gmlp, softmax_pv


POST_SPLIT = 4


def _post_phases(ya, yb, h_ref, p_ref, woa_ref, wob_ref, wple_ref, wg_ref, pn_ref, gn_ref, nn_ref, out_refs, final):
    cw = D_MODEL // POST_SPLIT
    st = {"h1": [], "gate": []}

    def out_proj(c):
        cols = pl.ds(c * cw, cw)
        st["h1"].append(h_ref[:, cols] + _dot(ya(), woa_ref[:, cols]) + _dot(yb(), wob_ref[:, cols]))

    def embed():
        st["e"] = _rms(_dot(p_ref[...].astype(BF16), wple_ref[...]), pn_ref[...])

    def norm():
        st["h1"] = jnp.concatenate(st["h1"], axis=1)
        st["hn1"] = _rms(st["h1"], gn_ref[...]).astype(BF16)

    def gate(c):
        st["gate"].append(jax.nn.sigmoid(_dot(st["hn1"], wg_ref[:, pl.ds(c * cw, cw)])))

    def finish():
        h2 = st["h1"] + st["e"] * jnp.concatenate(st["gate"], axis=1)
        normed = _rms(h2, nn_ref[...])
        if final:
            out_refs[0][...] = normed
        else:
            out_refs[0][...] = h2
            out_refs[1][...] = normed.astype(BF16)

    return out_proj, embed, norm, gate, finish


def _layer0_tail_kernel(sinks_ref, z_ref, kp_ref, vp_ref, vn_ref, wsp_ref, bsp_ref,
                        h_ref, p_ref, woa_ref, wob_ref, wple_ref, wg_ref, pn_ref, gn_ref, nn_ref,
                        h_out_ref, hn_out_ref, y_buf, *, tb):
    i = pl.program_id(0)

    @pl.when(i == 0)
    def _():
        y_buf[...] = jnp.zeros_like(y_buf)

    slot = lax.rem(i, 2)
    y_new, y_old = y_buf.at[slot], y_buf.at[1 - slot]
    out_proj, embed, norm, gate, finish = _post_phases(
        lambda: y_old[:, :HALF], lambda: y_old[:, HALF:], h_ref, p_ref, woa_ref, wob_ref, wple_ref, wg_ref,
        pn_ref, gn_ref, nn_ref, (h_out_ref, hn_out_ref), final=False)
    mix = [_mixer0_phases(sinks_ref, z_ref, kp_ref, vp_ref, vn_ref, wsp_ref, bsp_ref, y_new, i == 0, r)
           for r in range(tb // WINDOW)]
    assert len(mix) == 2 and POST_SPLIT == 4
    (scores0, gmlp0, pv0), (scores1, gmlp1, pv1) = mix
    for piece in (lambda: out_proj(0), lambda: out_proj(1), scores0, lambda: out_proj(2), lambda: out_proj(3),
                  gmlp0, embed, pv0, norm, lambda: gate(0), scores1, lambda: gate(1), gmlp1,
                  lambda: gate(2), pv1, lambda: gate(3), finish):
        piece()


def _layer0_tail(z, h, p, sinks, vnorm, w_sp, b_sp, w_out, w_ple, w_gate, ple_norm, gate_norm, next_norm, tb):
    s, d = h.shape
    n, nb = s // tb, tb // WINDOW
    cur = lambda i: jnp.minimum(i, n - 1)
    old = lambda i: jnp.maximum(i - 1, 0)
    prv = lambda c0: pl.BlockSpec((WINDOW, LANE), lambda i: (jnp.maximum(cur(i) * nb - 1, 0), c0 // LANE))
    full = lambda a: pl.BlockSpec(a.shape, lambda i: (0,) * a.ndim)
    const = lambda shape, r=0: pl.BlockSpec(shape, lambda i: (r, 0), pipeline_mode=pl.Buffered(1))
    per_layer = lambda rows: pl.BlockSpec((None, rows, d), lambda i: (0, 0, 0), pipeline_mode=pl.Buffered(1))
    row_block = pl.BlockSpec((tb, d), lambda i: (old(i), 0))
    vnorm = vnorm.reshape(1, HALF)
    b_t = b_sp.T
    return pl.pallas_call(
        functools.partial(_layer0_tail_kernel, tb=tb),
        grid=(n + 1,),
        in_specs=[pl.BlockSpec(memory_space=pltpu.SMEM),
                  pl.BlockSpec((tb, IN_EVEN), lambda i: (cur(i), 0)), prv(_A_K), prv(_A_V),
                  full(vnorm), full(w_sp), full(b_t),
                  row_block, pl.BlockSpec((None, tb, PLE_DIM), lambda i: (0, old(i), 0)),
                  const((HALF, d), 0), const((HALF, d), 1), per_layer(PLE_DIM), per_layer(d),
                  per_layer(1), per_layer(1), const((1, d))],
        out_specs=(row_block, row_block),
        out_shape=(jax.ShapeDtypeStruct((s, d), F32), jax.ShapeDtypeStruct((s, d), BF16)),
        scratch_shapes=[pltpu.VMEM((2, tb, d), BF16)],
        compiler_params=pltpu.CompilerParams(
            dimension_semantics=("arbitrary",), vmem_limit_bytes=VMEM_LIMIT),
        name="layer0_tail",
    )(sinks, z, z, z, vnorm, w_sp, b_t, h, p, w_out, w_out, w_ple, w_gate, ple_norm, gate_norm,
      next_norm.reshape(1, d))


def _chunk_cumsum(log_g, tb):
    t = lax.broadcasted_iota(jnp.int32, (tb, tb), 0)
    s = lax.broadcasted_iota(jnp.int32, (tb, tb), 1)
    tri = jnp.where((s <= t) & (t // LA_CHUNK == s // LA_CHUNK), 1.0, 0.0).astype(BF16)
    hi = log_g.astype(BF16)
    low = (log_g - hi.astype(F32)).astype(BF16)
    return _dot(tri, hi) + _dot(tri, low)


def _att_exact(qc, g, kbuf, gbuf, r0):
    row = lax.broadcasted_iota(jnp.int32, (LA_CHUNK, LA_CHUNK), 0)
    col = lax.broadcasted_iota(jnp.int32, (LA_CHUNK, LA_CHUNK), 1)
    sub_pos = lax.broadcasted_iota(jnp.int32, (LA_CHUNK, 1), 0) % LA_SUB
    zeros = lambda n: jnp.zeros((n, HEAD_K), F32)
    kc = kbuf[pl.ds(LA_SUB + r0, LA_CHUNK), :]
    att = jnp.zeros((LA_CHUNK, LA_CHUNK), F32)
    for d in range(LA_SUB):
        ks = kbuf[pl.ds(LA_SUB + r0 - d, LA_CHUNK), :]
        gs = gbuf[pl.ds(LA_SUB + r0 - d, LA_CHUNK), :]
        pd = jnp.where(sub_pos >= d, qc * ks * jnp.exp2(g - gs), 0.0)
        att = jnp.where(col == row - d, jnp.sum(pd, axis=-1, keepdims=True), att)
    m = LA_SUB
    while m < LA_CHUNK:
        q_parts, k_parts = [], []
        for b0 in range(0, LA_CHUNK, 2 * m):
            ref_row = g[b0 + m - 1:b0 + m]
            lower, upper = slice(b0, b0 + m), slice(b0 + m, b0 + 2 * m)
            q_parts += [zeros(m), qc[upper] * jnp.exp2(g[upper] - ref_row)]
            k_parts += [kc[lower] * jnp.exp2(ref_row - g[lower]), zeros(m)]
        part = _dot_nt(jnp.concatenate(q_parts, axis=0).astype(BF16),
                       jnp.concatenate(k_parts, axis=0).astype(BF16))
        if 2 * m < LA_CHUNK:
            part = jnp.where(row // (2 * m) == col // (2 * m), part, 0.0)
        att = att + part
        m *= 2
    return att


def _att_factored(qc, kc, g, q_in):
    row = lax.broadcasted_iota(jnp.int32, (LA_CHUNK, LA_CHUNK), 0)
    col = lax.broadcasted_iota(jnp.int32, (LA_CHUNK, LA_CHUNK), 1)
    zeros = lambda n: jnp.zeros((n, HEAD_K), BF16)
    q_slabs, k_slabs = [], []
    for b0 in range(0, LA_CHUNK, LA_SUB):
        blk = slice(b0, b0 + LA_SUB)
        ref_row = g[b0 - 1:b0] if b0 else 0.0
        q_part = (qc[b0:] * jnp.exp2(g[b0:] - ref_row)).astype(BF16) if b0 else q_in
        k_part = (kc[blk] * jnp.exp2(ref_row - g[blk])).astype(BF16)
        q_slabs.append(jnp.concatenate([zeros(b0), q_part], axis=0) if b0 else q_part)
        k_slabs.append(jnp.concatenate(
            [z for z in (zeros(b0), k_part, zeros(LA_CHUNK - LA_SUB - b0)) if z.shape[0]], axis=0))
    att = _dot_nt(jnp.concatenate(q_slabs, axis=1), jnp.concatenate(k_slabs, axis=1))
    return jnp.where(col <= row, att, 0.0)


def _linear_attention_heads(qs, ks, vs, log_g_fns, safe, st_ref, kbuf, gbuf, emit, tb):
    n_heads, n_chunks = len(qs), tb // LA_CHUNK
    chunk = lambda a, c: a[c * LA_CHUNK:(c + 1) * LA_CHUNK]

    def run(exact):
        gcs = [_chunk_cumsum(fn(), tb) for fn in log_g_fns]
        q_in = {(j, c): (chunk(qs[j], c) * jnp.exp2(chunk(gcs[j], c))).astype(BF16)
                for j in range(n_heads) for c in range(n_chunks)}
        att = {}
        for j in range(n_heads):
            if exact:
                kbuf[j, pl.ds(0, LA_SUB), :] = jnp.zeros((LA_SUB, HEAD_K), F32)
                gbuf[j, pl.ds(0, LA_SUB), :] = jnp.zeros((LA_SUB, HEAD_K), F32)
                kbuf[j, pl.ds(LA_SUB, tb), :] = ks[j]
                gbuf[j, pl.ds(LA_SUB, tb), :] = gcs[j]
            for c in range(n_chunks):
                if exact:
                    att[j, c] = _att_exact(chunk(qs[j], c), chunk(gcs[j], c), kbuf.at[j], gbuf.at[j], c * LA_CHUNK)
                else:
                    att[j, c] = _att_factored(chunk(qs[j], c), chunk(ks[j], c), chunk(gcs[j], c), q_in[j, c])
        intra, updates, decays = {}, {}, {}
        for j in range(n_heads):
            for c in range(n_chunks):
                kc, vc, g = chunk(ks[j], c), chunk(vs[j], c), chunk(gcs[j], c)
                g_last = g[LA_CHUNK - 1:LA_CHUNK]
                intra[j, c] = _dot(att[j, c].astype(BF16), vc)
                updates[j, c] = _dot_tn(vc, (kc * jnp.exp2(g_last - g)).astype(BF16))
                decays[j, c] = jnp.exp2(g_last)
        states = {}
        for j in range(n_heads):
            st = st_ref[j]
            for c in range(n_chunks):
                states[j, c] = st.astype(BF16)
                st = st * decays[j, c] + updates[j, c]
            st_ref[j] = st
        for j in range(n_heads):
            emit(j, jnp.concatenate(
                [_dot_nt(q_in[j, c], states[j, c]) + intra[j, c]
                 for c in range(n_chunks)], axis=0))

    pl.when(safe)(lambda: run(exact=False))
    pl.when(jnp.logical_not(safe))(lambda: run(exact=True))


def _la_scratch(tb, dv, hp):
    return [pltpu.VMEM((hp, dv, HEAD_K), F32),
            pltpu.VMEM((hp, LA_SUB + tb, HEAD_K), F32),
            pltpu.VMEM((hp, LA_SUB + tb, HEAD_K), F32)]


def _head_cols(j, width):
    return slice(j * width, (j + 1) * width)


def _gated_head_norm(y_ref, on_ref, gate_ref, width):
    def emit(j, o):
        cols = _head_cols(j, width)
        y_ref[:, cols] = (_rms(o, on_ref[:, cols]) * _silu(gate_ref[:, cols].astype(F32))).astype(BF16)
    return emit


def _hgrn2_kernel(q_ref, f_ref, i_ref, gate_ref, lb_ref, on_ref, y_ref, st_ref, kbuf, gbuf, *, tb, hp):
    @pl.when(pl.program_id(1) == 0)
    def _():
        st_ref[...] = jnp.zeros_like(st_ref)

    qs, ks, vs, fs = [], [], [], []
    for j in range(hp):
        cols = _head_cols(j, HEAD_K)
        lb = lb_ref[:, cols]
        f = lb + (1.0 - lb) * jax.nn.sigmoid(f_ref[:, cols].astype(F32))
        qs.append(q_ref[:, cols].astype(F32) * (HEAD_K ** -0.5))
        ks.append(1.0 - f)
        vs.append(i_ref[:, cols])
        fs.append(f)
    safe = jnp.min(functools.reduce(jnp.minimum, fs)) >= math.exp(-MAX_SAFE_STEP_DECAY)
    _linear_attention_heads(qs, ks, vs, [functools.partial(jnp.log2, f) for f in fs], safe, st_ref, kbuf, gbuf,
                            _gated_head_norm(y_ref, on_ref, gate_ref, HEAD_V_C), tb)


def _gla_kernel(q_ref, k_ref, v_ref, gate_ref, glr_ref, wup_ref, bg_ref, on_ref, y_ref,
                st_ref, kbuf, gbuf, *, tb, hp):
    @pl.when(pl.program_id(1) == 0)
    def _():
        st_ref[...] = jnp.zeros_like(st_ref)

    x = _dot(glr_ref[...], wup_ref[...]) + bg_ref[...]
    log_sigmoid = lambda t: -(jnp.maximum(-t, 0.0) + jnp.log1p(jnp.exp(-jnp.abs(t))))
    to_log2 = math.log2(math.e) / GATE_LOGIT_NORM_D
    qs, ks, vs, lgs = [], [], [], []
    for j in range(hp):
        cols = _head_cols(j, HEAD_K)
        qs.append(q_ref[:, cols].astype(F32) * (HEAD_K ** -0.5))
        ks.append(k_ref[:, cols].astype(F32))
        vs.append(v_ref[:, _head_cols(j, HEAD_V_D)])
        lgs.append(lambda cols=cols: log_sigmoid(x[:, cols]) * to_log2)
    safe = jnp.max(-x) <= MAX_SAFE_STEP_DECAY * GATE_LOGIT_NORM_D - math.log(2.0)
    _linear_attention_heads(qs, ks, vs, lgs, safe, st_ref, kbuf, gbuf,
                            _gated_head_norm(y_ref, on_ref, gate_ref, HEAD_V_D), tb)


_C_Q, _C_F, _C_I, _C_GATE = 0, 8, 16, 24
_D_Q, _D_K, _D_V, _D_GATE, _D_GLR = 32, 36, 40, 48, 56
PACKED_ODD = 57 * LANE


def _hgrn2(z, lb, onorm, tb, hp):
    s = z.shape[0]
    kw, vw = hp * HEAD_K, hp * HEAD_V_C
    col = lambda c0: pl.BlockSpec((tb, kw), lambda h, i: (i, c0 // hp + h))
    vec = lambda w: pl.BlockSpec((1, w), lambda h, i: (0, h))
    return pl.pallas_call(
        functools.partial(_hgrn2_kernel, tb=tb, hp=hp),
        grid=(N_HEADS_C // hp, s // tb),
        in_specs=[col(_C_Q), col(_C_F), col(_C_I), col(_C_GATE), vec(kw), vec(vw)],
        out_specs=pl.BlockSpec((tb, vw), lambda h, i: (i, h)),
        out_shape=jax.ShapeDtypeStruct((s, HALF), BF16),
        scratch_shapes=_la_scratch(tb, HEAD_V_C, hp),
        compiler_params=pltpu.CompilerParams(
            dimension_semantics=("arbitrary", "arbitrary"), vmem_limit_bytes=VMEM_LIMIT),
        name="hgrn2",
    )(z, z, z, z, lb.reshape(1, HALF), onorm.reshape(1, HALF))


def _gla(z, w_up, b_gate, onorm, tb, hp):
    s = z.shape[0]
    kw, vw = hp * HEAD_K, hp * HEAD_V_D
    col = lambda c0, w: pl.BlockSpec((tb, w), lambda h, i: (i, c0 * LANE // w + h))
    vec = lambda w: pl.BlockSpec((1, w), lambda h, i: (0, h))
    w_up = jnp.pad(w_up, ((0, LANE - GATE_RANK_D), (0, 0))).astype(BF16)
    return pl.pallas_call(
        functools.partial(_gla_kernel, tb=tb, hp=hp),
        grid=(N_HEADS_D // hp, s // tb),
        in_specs=[col(_D_Q, kw), col(_D_K, kw), col(_D_V, vw), col(_D_GATE, vw),
                  pl.BlockSpec((tb, LANE), lambda h, i: (i, _D_GLR)),
                  pl.BlockSpec((LANE, kw), lambda h, i: (0, h)),
                  vec(kw), vec(vw)],
        out_specs=pl.BlockSpec((tb, vw), lambda h, i: (i, h)),
        out_shape=jax.ShapeDtypeStruct((s, HALF), BF16),
        scratch_shapes=_la_scratch(tb, HEAD_V_D, hp),
        compiler_params=pltpu.CompilerParams(
            dimension_semantics=("arbitrary", "arbitrary"), vmem_limit_bytes=VMEM_LIMIT),
        name="gla",
    )(z, z, z, z, z, w_up, b_gate.reshape(1, N_HEADS_D * HEAD_K), onorm.reshape(1, HALF))


def _post_kernel(ya_ref, yb_ref, h_ref, p_ref, woa_ref, wob_ref, wple_ref, wg_ref,
                 pn_ref, gn_ref, nn_ref, *out_refs, final):
    out_proj, embed, norm, gate, finish = _post_phases(
        lambda: ya_ref[...], lambda: yb_ref[...], h_ref, p_ref, woa_ref, wob_ref, wple_ref, wg_ref,
        pn_ref, gn_ref, nn_ref, out_refs, final)
    for c in range(POST_SPLIT):
        out_proj(c)
    embed()
    norm()
    for c in range(POST_SPLIT):
        gate(c)
    finish()


def _post(ya, yb, ca, cb, h, p, layer, w_out, w_ple, w_gate, ple_norm, gate_norm, next_norm, tm, final):
    s, d = h.shape
    const = lambda shape, r=0: pl.BlockSpec(shape, lambda i: (r, 0), pipeline_mode=pl.Buffered(1))
    per_layer = lambda rows: pl.BlockSpec((None, rows, d), lambda i: (layer, 0, 0), pipeline_mode=pl.Buffered(1))
    row_block = pl.BlockSpec((tm, d), lambda i: (i, 0))
    out_f32, out_bf16 = jax.ShapeDtypeStruct((s, d), F32), jax.ShapeDtypeStruct((s, d), BF16)
    return pl.pallas_call(
        functools.partial(_post_kernel, final=final),
        grid=(s // tm,),
        in_specs=[pl.BlockSpec((tm, HALF), lambda i: (i, ca)),
                  pl.BlockSpec((tm, HALF), lambda i: (i, cb)),
                  row_block,
                  pl.BlockSpec((None, tm, PLE_DIM), lambda i: (layer, i, 0)),
                  const((HALF, d), 0), const((HALF, d), 1), per_layer(PLE_DIM), per_layer(d),
                  per_layer(1), per_layer(1), const((1, d))],
        out_specs=row_block if final else (row_block, row_block),
        out_shape=out_f32 if final else (out_f32, out_bf16),
        compiler_params=pltpu.CompilerParams(
            dimension_semantics=("arbitrary",), vmem_limit_bytes=VMEM_LIMIT),
        name="post_final" if final else "post",
    )(ya, yb, h, p, w_out, w_out, w_ple, w_gate, ple_norm, gate_norm, next_norm.reshape(1, d))


PACK_ROWS = 3 * LANE
_PACK_SHIFTED = _D_GATE * LANE // PACK_ROWS
_PACK_LAST = PACKED_ODD // PACK_ROWS - 1


def _pack_odd_kernel(a_ref, b_ref, c_ref, o_ref):
    j = pl.program_id(0)

    @pl.when(j < _PACK_SHIFTED)
    def _():
        o_ref[...] = a_ref[...].astype(BF16)

    @pl.when((j >= _PACK_SHIFTED) & (j < _PACK_LAST))
    def _():
        o_ref[...] = jnp.concatenate([a_ref[GATE_RANK_D:, :], b_ref[:GATE_RANK_D, :]], axis=0).astype(BF16)

    @pl.when(j == _PACK_LAST)
    def _():
        n_gate = _D_GLR * LANE - _PACK_LAST * PACK_ROWS
        pad = jnp.zeros((PACK_ROWS - n_gate - GATE_RANK_D, a_ref.shape[1]), F32)
        o_ref[...] = jnp.concatenate(
            [a_ref[GATE_RANK_D:GATE_RANK_D + n_gate, :], c_ref[:GATE_RANK_D, :], pad], axis=0).astype(BF16)


def _pack_odd(w_t):
    n, d = w_t.shape
    next_block = lambda j: jnp.clip(j + 1, _PACK_SHIFTED + 1, pl.cdiv(n, PACK_ROWS) - 1)
    return pl.pallas_call(
        _pack_odd_kernel,
        grid=(_PACK_LAST + 1,),
        in_specs=[pl.BlockSpec((PACK_ROWS, d), lambda j: (j, 0)),
                  pl.BlockSpec((PACK_ROWS, d), lambda j: (next_block(j), 0)),
                  pl.BlockSpec((PACK_ROWS, d), lambda j: (_PACK_SHIFTED, 0))],
        out_specs=pl.BlockSpec((PACK_ROWS, d), lambda j: (j, 0)),
        out_shape=jax.ShapeDtypeStruct((PACKED_ODD, d), BF16),
        compiler_params=pltpu.CompilerParams(
            dimension_semantics=("arbitrary",), vmem_limit_bytes=VMEM_LIMIT),
        name="pack_odd",
    )(w_t, w_t, w_t)


def kernel(x, p, norm_mix, w_in_even, sinks_a, vnorm_b, w_spatial_b, b_spatial_b, w_out_even,
           w_in_odd, lower_bounds_c, onorm_c, w_gate_up_d, b_gate_d, onorm_d, w_out_odd,
           w_ple_proj, ple_norm, ple_gate_norm, w_ple_gate, final_norm):
    assert x.shape[0] == 1 and x.shape[2] == D_MODEL and norm_mix.shape[0] == 2
    s = x.shape[1]
    h = x[0]
    p = p.reshape(2, s, PLE_DIM)
    w_ple, w_gate = w_ple_proj.astype(BF16), w_ple_gate.astype(BF16)
    ple_norm, gate_norm = ple_norm.reshape(2, 1, D_MODEL), ple_gate_norm.reshape(2, 1, D_MODEL)
    post = functools.partial(_post, p=p, w_ple=w_ple, w_gate=w_gate, ple_norm=ple_norm, gate_norm=gate_norm,
                             tm=256)

    z = _norm_matmul(h, norm_mix[0], w_in_even[0].astype(BF16), tm=min(s, 1024), tn=IN_EVEN // 3)
    h, hn = _layer0_tail(z, h, p, sinks_a[0], vnorm_b[0], w_spatial_b[0], b_spatial_b[0],
                         w_out_even[0].astype(BF16), w_ple, w_gate, ple_norm, gate_norm, norm_mix[1], tb=256)

    z = _norm_matmul(hn, None, _pack_odd(w_in_odd[0].T), tm=min(s, 1024), tn=PACKED_ODD // 3)
    sm = jax.nn.softmax(lower_bounds_c.astype(F32), axis=0)
    lb = (jnp.cumsum(sm, axis=0) - sm[0])[1]
    yc = _hgrn2(z, lb, onorm_c[0], tb=256, hp=8)
    yd = _gla(z, w_gate_up_d[0], b_gate_d[0], onorm_d[0], tb=256, hp=4)
    out = post(yc, yd, 0, 0, h, layer=1, w_out=w_out_odd[0].astype(BF16), next_norm=final_norm, final=True)
    return out[None]
```
